```python
import math
import jax
import jax.numpy as jnp
from jax import lax
import numpy as np

D_MODEL = 2048
BATCH = 16
SEQ = 2048
DEPTH = 2

HEAD_DIM = 128
MIX_WIDTH = D_MODEL
A_GROUPS = MIX_WIDTH // (2 * HEAD_DIM)
A_WIDTH = A_GROUPS * HEAD_DIM
CONV_WIDTH = 3
B_HEADS = MIX_WIDTH // (2 * HEAD_DIM)
B_WIDTH = B_HEADS * HEAD_DIM
Q_RANK = 512
KV_RANK = 256
IDX_HEADS = 16
IDX_DIM = 64
IDX_SCALE = (IDX_DIM ** -0.5) * (IDX_HEADS ** -0.5)
TOPK_MAX = 256
C_GROUPS = MIX_WIDTH // (2 * HEAD_DIM)
C_WIDTH = C_GROUPS * HEAD_DIM
CHUNK = 128
D_HEADS = MIX_WIDTH // (2 * HEAD_DIM)
D_WIDTH = D_HEADS * HEAD_DIM
REL_BUCKETS = 32
REL_MAX_DIST = 128
D_FF = 5632
N_EXPERTS = 8
TOP_K = 2
D_FF_EXPERT = 7168
Q_BLOCK = 128
EPS = 1e-6

EVEN_SPLITS = (A_WIDTH, 2 * A_WIDTH, 3 * A_WIDTH, 3 * A_WIDTH + Q_RANK,
               3 * A_WIDTH + Q_RANK + KV_RANK, 3 * A_WIDTH + Q_RANK + KV_RANK + IDX_DIM)
EVEN_IN = 3 * A_WIDTH + Q_RANK + KV_RANK + IDX_DIM + IDX_HEADS
ODD_SPLITS = (C_WIDTH, 2 * C_WIDTH, 2 * C_WIDTH + D_WIDTH, 2 * C_WIDTH + 2 * D_WIDTH)
ODD_IN = 2 * C_WIDTH + 3 * D_WIDTH
N_EVEN = (DEPTH + 1) // 2
N_ODD = DEPTH // 2

kernel_name = 'hybrid_conv_dsa_gmlp_stickbreak_moe'


def rmsnorm(x, g):
    xf = x.astype(jnp.float32)
    y = xf * lax.rsqrt(jnp.mean(xf * xf, axis=-1, keepdims=True) + EPS)
    return (y * g.astype(jnp.float32)).astype(x.dtype)


def swiglu(h, w_gate, w_up, w_down):
    return (jax.nn.silu(h @ w_gate) * (h @ w_up)) @ w_down


def rel_bucket(dist):
    n = jnp.maximum(dist, 0)
    max_exact = REL_BUCKETS // 2
    nf = jnp.maximum(n, max_exact).astype(jnp.float32)
    large = max_exact + (jnp.log(nf / max_exact) / math.log(REL_MAX_DIST / max_exact)
                         * (REL_BUCKETS - max_exact)).astype(jnp.int32)
    large = jnp.minimum(large, REL_BUCKETS - 1)
    return jnp.where(n < max_exact, n, large)


def short_conv_mixer(b_gate, c_gate, xv, conv_w):
    seq = xv.shape[1]
    z = c_gate * xv
    zp = jnp.pad(z, ((0, 0), (CONV_WIDTH - 1, 0), (0, 0)))
    y = conv_w[0] * zp[:, 0:seq]
    for j in range(1, CONV_WIDTH):
        y = y + conv_w[j] * zp[:, j:j + seq]
    return b_gate * y


def dsa_mixer(c_q, c_kv, k_idx, w_idx, g_cq, g_ckv, w_uq, w_uk, w_uv, w_qidx, rel_bias):
    bsz, seq, _ = c_q.shape
    c_q = rmsnorm(c_q, g_cq)
    c_kv = rmsnorm(c_kv, g_ckv)
    q = jnp.einsum('bsr,rhd->bshd', c_q, w_uq)
    q_lat = jnp.einsum('bshd,chd->bshc', q, w_uk)
    q_idx = jnp.einsum('bsr,rgd->bsgd', c_q, w_qidx)
    topk = min(TOPK_MAX, seq // 4)
    scale = HEAD_DIM ** -0.5
    outs = []
    for blk in range(seq // Q_BLOCK):
        t0 = blk * Q_BLOCK
        t1 = t0 + Q_BLOCK
        kk = min(topk, t1)
        tpos = t0 + jnp.arange(Q_BLOCK, dtype=jnp.int32)
        spos = jnp.arange(t1, dtype=jnp.int32)
        s_idx = jax.nn.relu(jnp.einsum('btgd,bsd->btgs', q_idx[:, t0:t1],
                                       k_idx[:, :t1]).astype(jnp.float32))
        score_raw = jnp.einsum('btgs,btg->bts', s_idx,
                               w_idx[:, t0:t1].astype(jnp.float32)) * IDX_SCALE
        score = jnp.where((spos[None, :] <= tpos[:, None])[None], score_raw, -jnp.inf)
        _, sel = lax.top_k(score, kk)
        valid = sel <= tpos[None, :, None]
        idx_sel = jnp.take_along_axis(score_raw, sel, axis=-1)
        ckv_sel = jax.vmap(lambda c, i: c[i])(c_kv[:, :t1], sel)
        logits = jnp.einsum('bthc,btkc->bhtk', q_lat[:, t0:t1], ckv_sel).astype(jnp.float32) * scale
        bias = rel_bias[rel_bucket(tpos[None, :, None] - sel)].astype(jnp.float32)
        logits = logits + jnp.transpose(bias, (0, 3, 1, 2)) + idx_sel[:, None]
        logits = jnp.where(valid[:, None], logits, -jnp.inf)
        p = jax.nn.softmax(logits, axis=-1).astype(ckv_sel.dtype)
        o_lat = jnp.einsum('bhtk,btkc->bthc', p, ckv_sel)
        outs.append(jnp.einsum('bthc,chd->bthd', o_lat, w_uv))
    return jnp.concatenate(outs, axis=1).reshape(bsz, seq, B_WIDTH)


def chunked_spatial_gating(u, v, g_v, w_s, b_s):
    bsz, seq, _ = v.shape
    v = rmsnorm(v, g_v)
    vc = v.reshape(bsz, seq // CHUNK, CHUNK, C_GROUPS, C_WIDTH // C_GROUPS)
    tril = jnp.tril(jnp.ones((CHUNK, CHUNK), dtype=bool))
    ws = jnp.where(tril[None], w_s, 0.0).astype(v.dtype)
    mixed = jnp.einsum('gts,bnsgc->bntgc', ws, vc) + jnp.transpose(b_s)[None, None, :, :, None]
    return u * mixed.reshape(bsz, seq, C_WIDTH)


def stick_breaking(q, k, v):
    bsz, seq, nh, hd = q.shape
    scale = hd ** -0.5
    outs = []
    for blk in range(seq // Q_BLOCK):
        t0 = blk * Q_BLOCK
        t1 = t0 + Q_BLOCK
        tpos = t0 + jnp.arange(Q_BLOCK, dtype=jnp.int32)
        spos = jnp.arange(t1, dtype=jnp.int32)
        past = (spos[None, :] < tpos[:, None])[None, None]
        z = jnp.einsum('bthd,bshd->bhts', q[:, t0:t1], k[:, :t1]).astype(jnp.float32) * scale
        log_keep = jnp.where(past, -jax.nn.softplus(z), 0.0)
        suffix = lax.cumsum(log_keep, axis=3, reverse=True)
        excl = jnp.concatenate([suffix[..., 1:], jnp.zeros_like(suffix[..., :1])], axis=-1)
        weights = jnp.where(past, jnp.exp(jax.nn.log_sigmoid(z) + excl), 0.0)
        outs.append(jnp.einsum('bhts,bshd->bthd', weights.astype(v.dtype), v[:, :t1]))
    return jnp.concatenate(outs, axis=1).reshape(bsz, seq, nh * hd)


def moe_swiglu(h, w_router, w_gate, w_up, w_down):
    logits = jnp.einsum('bsd,de->bse', h, w_router).astype(jnp.float32)
    top_val, top_idx = lax.top_k(logits, TOP_K)
    gates = jax.nn.softmax(top_val, axis=-1)
    combine = jnp.sum(jax.nn.one_hot(top_idx, N_EXPERTS, dtype=jnp.float32) * gates[..., None], axis=-2)
    combine = combine.astype(h.dtype)
    out = jnp.zeros_like(h)
    for e in range(N_EXPERTS):
        out = out + combine[..., e:e + 1] * swiglu(h, w_gate[e], w_up[e], w_down[e])
    return out


def even_layer(x, rel_bias, norm_mix, w_in, conv_w, norm_cq, norm_ckv, w_uq, w_uk, w_uv, w_qidx,
               w_out, norm_ffn, ffn_gate, ffn_up, ffn_down):
    h = rmsnorm(x, norm_mix)
    proj = h @ w_in
    a_b, a_c, a_x, c_q, c_kv, k_idx, w_idx = jnp.split(proj, EVEN_SPLITS, axis=-1)
    y_a = short_conv_mixer(a_b, a_c, a_x, conv_w)
    y_b = dsa_mixer(c_q, c_kv, k_idx, w_idx, norm_cq, norm_ckv, w_uq, w_uk, w_uv, w_qidx, rel_bias)
    x = x + jnp.concatenate([y_a, y_b], axis=-1) @ w_out
    h = rmsnorm(x, norm_ffn)
    return x + swiglu(h, ffn_gate, ffn_up, ffn_down)


def odd_layer(x, norm_mix, w_in, norm_v, w_s, b_s, w_out, norm_ffn, router, exp_gate, exp_up, exp_down):
    bsz, seq, _ = x.shape
    h = rmsnorm(x, norm_mix)
    proj = h @ w_in
    c_u, c_v, d_q, d_k, d_v = jnp.split(proj, ODD_SPLITS, axis=-1)
    y_c = chunked_spatial_gating(jax.nn.gelu(c_u), jax.nn.gelu(c_v), norm_v, w_s, b_s)
    y_d = stick_breaking(d_q.reshape(bsz, seq, D_HEADS, HEAD_DIM),
                         d_k.reshape(bsz, seq, D_HEADS, HEAD_DIM),
                         d_v.reshape(bsz, seq, D_HEADS, HEAD_DIM))
    x = x + jnp.concatenate([y_c, y_d], axis=-1) @ w_out
    h = rmsnorm(x, norm_ffn)
    return x + moe_swiglu(h, router, exp_gate, exp_up, exp_down)


def setup_inputs(seed: int = 0) -> dict:
    key = jax.random.key(seed)
    keys = jax.random.split(key, 32)
    counter = [0]

    def nxt():
        k = keys[counter[0]]
        counter[0] += 1
        return k

    def nrm(shape, fan_in):
        return jax.random.normal(nxt(), shape, jnp.float32) * (fan_in ** -0.5)

    def gain(shape):
        return 1.0 + 0.1 * jax.random.normal(nxt(), shape, jnp.float32)

    ne, no = N_EVEN, N_ODD
    return {
        'x': jax.random.normal(nxt(), (BATCH, SEQ, D_MODEL), jnp.float32),
        'rel_bias': 0.5 * jax.random.normal(nxt(), (REL_BUCKETS, B_HEADS), jnp.float32),
        'final_norm': gain((D_MODEL,)),
        'e_norm_mix': gain((ne, D_MODEL)),
        'e_w_in': nrm((ne, D_MODEL, EVEN_IN), D_MODEL),
        'e_conv_w': nrm((ne, CONV_WIDTH, A_WIDTH), CONV_WIDTH),
        'e_norm_cq': gain((ne, Q_RANK)),
        'e_norm_ckv': gain((ne, KV_RANK)),
        'e_w_uq': nrm((ne, Q_RANK, B_HEADS, HEAD_DIM), Q_RANK),
        'e_w_uk': nrm((ne, KV_RANK, B_HEADS, HEAD_DIM), KV_RANK),
        'e_w_uv': nrm((ne, KV_RANK, B_HEADS, HEAD_DIM), KV_RANK),
        'e_w_qidx': nrm((ne, Q_RANK, IDX_HEADS, IDX_DIM), Q_RANK),
        'e_w_out': nrm((ne, A_WIDTH + B_WIDTH, D_MODEL), A_WIDTH + B_WIDTH),
        'e_norm_ffn': gain((ne, D_MODEL)),
        'e_ffn_gate': nrm((ne, D_MODEL, D_FF), D_MODEL),
        'e_ffn_up': nrm((ne, D_MODEL, D_FF), D_MODEL),
        'e_ffn_down': nrm((ne, D_FF, D_MODEL), D_FF),
        'o_norm_mix': gain((no, D_MODEL)),
        'o_w_in': nrm((no, D_MODEL, ODD_IN), D_MODEL),
        'o_norm_v': gain((no, C_WIDTH)),
        'o_w_s': nrm((no, C_GROUPS, CHUNK, CHUNK), CHUNK),
        'o_b_s': gain((no, C_GROUPS, CHUNK)),
        'o_w_out': nrm((no, C_WIDTH + D_WIDTH, D_MODEL), C_WIDTH + D_WIDTH),
        'o_norm_ffn': gain((no, D_MODEL)),
        'o_router': nrm((no, D_MODEL, N_EXPERTS), D_MODEL),
        'o_exp_gate': nrm((no, N_EXPERTS, D_MODEL, D_FF_EXPERT), D_MODEL),
        'o_exp_up': nrm((no, N_EXPERTS, D_MODEL, D_FF_EXPERT), D_MODEL),
        'o_exp_down': nrm((no, N_EXPERTS, D_FF_EXPERT, D_MODEL), D_FF_EXPERT),
    }


def reference(x, rel_bias, final_norm,
              e_norm_mix, e_w_in, e_conv_w, e_norm_cq, e_norm_ckv, e_w_uq, e_w_uk, e_w_uv, e_w_qidx,
              e_w_out, e_norm_ffn, e_ffn_gate, e_ffn_up, e_ffn_down,
              o_norm_mix, o_w_in, o_norm_v, o_w_s, o_b_s, o_w_out, o_norm_ffn, o_router,
              o_exp_gate, o_exp_up, o_exp_down):
    for layer in range(DEPTH):
        i = layer // 2
        if layer % 2 == 0:
            x = even_layer(x, rel_bias, e_norm_mix[i], e_w_in[i], e_conv_w[i], e_norm_cq[i], e_norm_ckv[i],
                           e_w_uq[i], e_w_uk[i], e_w_uv[i], e_w_qidx[i], e_w_out[i], e_norm_ffn[i],
                           e_ffn_gate[i], e_ffn_up[i], e_ffn_down[i])
        else:
            x = odd_layer(x, o_norm_mix[i], o_w_in[i], o_norm_v[i], o_w_s[i], o_b_s[i], o_w_out[i],
                          o_norm_ffn[i], o_router[i], o_exp_gate[i], o_exp_up[i], o_exp_down[i])
    return rmsnorm(x, final_norm)
```

```python
import functools
import math

import jax
import jax.numpy as jnp
from jax import lax
from jax.experimental import pallas as pl
from jax.experimental.pallas import tpu as pltpu

F32 = jnp.float32
BF16 = jnp.bfloat16

EPS = 1e-6
LANES = 128
HEAD_DIM = 128
Q_RANK = 512
KV_RANK = 256
IDX_HEADS = 16
IDX_DIM = 64
IDX_SCALE = (IDX_DIM ** -0.5) * (IDX_HEADS ** -0.5)
TOPK_MAX = 256
REL_BUCKETS = 32
REL_MAX_DIST = 128
CONV_WIDTH = 3
CHUNK = 128
TOP_K = 2
INT_MIN = -(2 ** 31)
MIB = 1024 * 1024


def _params(sem, vmem_mib):
    return pltpu.CompilerParams(dimension_semantics=sem, vmem_limit_bytes=vmem_mib * MIB)


def _rms(x, g):
    return x * lax.rsqrt(jnp.mean(x * x, axis=-1, keepdims=True) + EPS) * g


def _dot(a, b):
    return jnp.dot(a, b, preferred_element_type=F32)


def _dot_nt(a, b):
    return lax.dot_general(a, b, (((1,), (1,)), ((), ())), preferred_element_type=F32)


def _norm_matmul_kernel(x_ref, g_ref, w_ref, o_ref, h_ref):
    @pl.when(pl.program_id(1) == 0)
    def _():
        h_ref[...] = _rms(x_ref[...], g_ref[...]).astype(BF16)

    o_ref[...] = _dot(h_ref[...], w_ref[...]).astype(o_ref.dtype)


def norm_matmul(x, g, w, tm=512, tn=1024):
    n, d = x.shape
    nout = w.shape[1]
    return pl.pallas_call(
        _norm_matmul_kernel,
        grid=(n // tm, nout // tn),
        in_specs=[pl.BlockSpec((tm, d), lambda i, j: (i, 0)),
                  pl.BlockSpec((1, d), lambda i, j: (0, 0)),
                  pl.BlockSpec((d, tn), lambda i, j: (0, j))],
        out_specs=pl.BlockSpec((tm, tn), lambda i, j: (i, j)),
        out_shape=jax.ShapeDtypeStruct((n, nout), BF16),
        scratch_shapes=[pltpu.VMEM((tm, d), BF16)],
        compiler_params=_params(("parallel", "arbitrary"), 40),
        name="norm_matmul",
    )(x, g.reshape(1, d), w)


def _conv_kernel(b_ref, c_ref, x_ref, w_ref, o_ref):
    z = c_ref[0].astype(F32) * x_ref[0].astype(F32)
    row = lax.broadcasted_iota(jnp.int32, z.shape, 0)
    y = w_ref[CONV_WIDTH - 1:CONV_WIDTH, :] * z
    for lag in range(1, CONV_WIDTH):
        zl = jnp.where(row >= lag, pltpu.roll(z, lag, 0), 0.0)
        y = y + w_ref[CONV_WIDTH - 1 - lag:CONV_WIDTH - lag, :] * zl
    o_ref[0] = (b_ref[0].astype(F32) * y).astype(o_ref.dtype)


def conv_mixer(proj, conv_w, a_width, tc=256):
    bsz, seq, _ = proj.shape
    nb = a_width // tc
    return pl.pallas_call(
        _conv_kernel,
        grid=(bsz, nb),
        in_specs=[pl.BlockSpec((1, seq, tc), lambda b, c: (b, 0, c)),
                  pl.BlockSpec((1, seq, tc), lambda b, c: (b, 0, nb + c)),
                  pl.BlockSpec((1, seq, tc), lambda b, c: (b, 0, 2 * nb + c)),
                  pl.BlockSpec((CONV_WIDTH, tc), lambda b, c: (0, c))],
        out_specs=pl.BlockSpec((1, seq, tc), lambda b, c: (b, 0, c)),
        out_shape=jax.ShapeDtypeStruct((bsz, seq, a_width), BF16),
        compiler_params=_params(("parallel", "parallel"), 40),
        name="conv_mixer",
    )(proj, proj, proj, conv_w)


def _dsa_prep_kernel(cq_ref, ckv_ref, gq_ref, gkv_ref, wuq_ref, wukt_ref, wqi_ref,
                     qlat_ref, qidx_ref, ckvo_ref):
    cqn = _rms(cq_ref[0].astype(F32), gq_ref[...]).astype(BF16)
    ckvo_ref[0] = _rms(ckv_ref[0].astype(F32), gkv_ref[...]).astype(BF16)
    q = _dot(cqn, wuq_ref[...]).astype(BF16)
    scale = HEAD_DIM ** -0.5
    for h in range(wukt_ref.shape[0]):
        qh = q[:, h * HEAD_DIM:(h + 1) * HEAD_DIM]
        qlat_ref[0, h] = (_dot(qh, wukt_ref[h]) * scale).astype(BF16)
    qi = _dot(cqn, wqi_ref[...]).astype(BF16)
    for g in range(IDX_HEADS):
        qidx_ref[0, g] = qi[:, g * LANES:(g + 1) * LANES]


def dsa_prep(proj, g_cq, g_ckv, w_uq, w_ukt, w_qidx_pad, cq_col, ckv_col, tm=256):
    bsz, seq, _ = proj.shape
    nh = w_ukt.shape[0]
    full = lambda *shape: pl.BlockSpec(shape, lambda b, i: (0,) * len(shape))
    return pl.pallas_call(
        _dsa_prep_kernel,
        grid=(bsz, seq // tm),
        in_specs=[pl.BlockSpec((1, tm, Q_RANK), lambda b, i: (b, i, cq_col // Q_RANK)),
                  pl.BlockSpec((1, tm, KV_RANK), lambda b, i: (b, i, ckv_col // KV_RANK)),
                  full(1, Q_RANK), full(1, KV_RANK),
                  full(Q_RANK, nh * HEAD_DIM), full(nh, HEAD_DIM, KV_RANK),
                  full(Q_RANK, IDX_HEADS * LANES)],
        out_specs=[pl.BlockSpec((1, nh, tm, KV_RANK), lambda b, i: (b, 0, i, 0)),
                   pl.BlockSpec((1, IDX_HEADS, tm, LANES), lambda b, i: (b, 0, i, 0)),
                   pl.BlockSpec((1, tm, KV_RANK), lambda b, i: (b, i, 0))],
        out_shape=[jax.ShapeDtypeStruct((bsz, nh, seq, KV_RANK), BF16),
                   jax.ShapeDtypeStruct((bsz, IDX_HEADS, seq, LANES), BF16),
                   jax.ShapeDtypeStruct((bsz, seq, KV_RANK), BF16)],
        compiler_params=_params(("parallel", "parallel"), 40),
        name="dsa_prep",
    )(proj, proj, g_cq.reshape(1, -1), g_ckv.reshape(1, -1), w_uq, w_ukt, w_qidx_pad)


def _dsa_kernel(qlat_ref, qidx_ref, widx_ref, kidx_ref, ckv_ref, toep_ref, wuv_ref, o_ref,
                sc_ref, key_ref, p_ref, m_ref, l_ref, acc_ref, *, topk):
    tq = LANES
    nh = qlat_ref.shape[1]
    i = pl.program_id(1)
    row = lax.broadcasted_iota(jnp.int32, (tq, tq), 0)
    col = lax.broadcasted_iota(jnp.int32, (tq, tq), 1)
    w = widx_ref[0].astype(F32) * IDX_SCALE
    qi = qidx_ref[0].reshape(IDX_HEADS * tq, LANES)

    def score_body(j, carry):
        kj = kidx_ref[0, pl.ds(pl.multiple_of(j * tq, tq), tq), :]
        sg = _dot_nt(qi, kj)
        s = jnp.zeros((tq, tq), F32)
        for g in range(IDX_HEADS):
            s = s + w[:, g:g + 1] * jnp.maximum(sg[g * tq:(g + 1) * tq], 0.0)
        sc_ref[j] = s
        bits = pltpu.bitcast(s, jnp.int32)
        key = bits ^ ((bits >> 31) & 0x7FFFFFFF)
        causal = (j * tq + col) <= (i * tq + row)
        key_ref[j] = jnp.where(causal, key, INT_MIN)
        return carry

    lax.fori_loop(0, i + 1, score_body, 0)

    kvec = jnp.minimum(topk, i * tq + row[:, :1] + 1).astype(F32)

    def count_ge(cand):
        def body(j, acc):
            return acc + jnp.where(key_ref[j] >= cand, 1.0, 0.0)
        acc = lax.fori_loop(0, i + 1, body, jnp.zeros((tq, tq), F32))
        return jnp.sum(acc, axis=1, keepdims=True)

    thr = jnp.where(count_ge(jnp.zeros((tq, 1), jnp.int32)) >= kvec, 0, INT_MIN).astype(jnp.int32)

    def bit_body(b, thr):
        cand = thr | jnp.left_shift(jnp.int32(1), 30 - b)
        return jnp.where(count_ge(cand) >= kvec, cand, thr)

    thr = lax.fori_loop(0, 31, bit_body, thr)

    m_ref[...] = jnp.full(m_ref.shape, -jnp.inf, F32)
    l_ref[...] = jnp.zeros(l_ref.shape, F32)
    acc_ref[...] = jnp.zeros(acc_ref.shape, F32)
    q = qlat_ref[0].reshape(nh * tq, KV_RANK)

    def att_body(j, carry):
        cj = ckv_ref[0, pl.ds(pl.multiple_of(j * tq, tq), tq), :]
        lg = _dot_nt(q, cj)
        sel = key_ref[j] >= thr
        sj = sc_ref[j]
        mdist = jnp.minimum(i - j, 2)
        for h in range(nh):
            rows = slice(h * tq, (h + 1) * tq)
            lh = jnp.where(sel, lg[rows] + toep_ref[h, mdist] + sj, -jnp.inf)
            m_old = m_ref[rows]
            m_new = jnp.maximum(m_old, jnp.max(lh, axis=1, keepdims=True))
            m_safe = jnp.where(m_new == -jnp.inf, 0.0, m_new)
            alpha = jnp.exp(m_old - m_safe)
            p = jnp.exp(lh - m_safe)
            m_ref[rows] = m_new
            l_ref[rows] = alpha * l_ref[rows] + jnp.sum(p, axis=1, keepdims=True)
            acc_ref[rows] = alpha * acc_ref[rows]
            p_ref[rows] = p.astype(BF16)
        acc_ref[...] += _dot(p_ref[...], cj)
        return carry

    lax.fori_loop(0, i + 1, att_body, 0)

    for h in range(nh):
        rows = slice(h * tq, (h + 1) * tq)
        o_lat = (acc_ref[rows] / l_ref[rows]).astype(BF16)
        o_ref[0, :, h * HEAD_DIM:(h + 1) * HEAD_DIM] = _dot(o_lat, wuv_ref[h]).astype(o_ref.dtype)


def dsa_attention(qlat, qidx, proj, ckv, toep, w_uv, kidx_col, widx_col, topk):
    bsz, nh, seq, _ = qlat.shape
    tq = LANES
    nkb = seq // tq
    full = lambda *shape: pl.BlockSpec(shape, lambda b, i: (0,) * len(shape))
    return pl.pallas_call(
        functools.partial(_dsa_kernel, topk=topk),
        grid=(bsz, seq // tq),
        in_specs=[pl.BlockSpec((1, nh, tq, KV_RANK), lambda b, i: (b, 0, i, 0)),
                  pl.BlockSpec((1, IDX_HEADS, tq, LANES), lambda b, i: (b, 0, i, 0)),
                  pl.BlockSpec((1, tq, LANES), lambda b, i: (b, i, widx_col // LANES)),
                  pl.BlockSpec((1, seq, LANES), lambda b, i: (b, 0, kidx_col // LANES)),
                  pl.BlockSpec((1, seq, KV_RANK), lambda b, i: (b, 0, 0)),
                  full(nh, 3, tq, tq),
                  full(nh, KV_RANK, HEAD_DIM)],
        out_specs=pl.BlockSpec((1, tq, nh * HEAD_DIM), lambda b, i: (b, i, 0)),
        out_shape=jax.ShapeDtypeStruct((bsz, seq, nh * HEAD_DIM), BF16),
        scratch_shapes=[pltpu.VMEM((nkb, tq, tq), F32),
                        pltpu.VMEM((nkb, tq, tq), jnp.int32),
                        pltpu.VMEM((nh * tq, tq), BF16),
                        pltpu.VMEM((nh * tq, 1), F32),
                        pltpu.VMEM((nh * tq, 1), F32),
                        pltpu.VMEM((nh * tq, KV_RANK), F32)],
        compiler_params=_params(("parallel", "arbitrary"), 40),
        name="dsa_attention",
    )(qlat, qidx, proj, proj, ckv, toep, w_uv)


def _rel_bucket(dist):
    n = jnp.maximum(dist, 0)
    max_exact = REL_BUCKETS // 2
    nf = jnp.maximum(n, max_exact).astype(F32)
    large = max_exact + (jnp.log(nf / max_exact) / math.log(REL_MAX_DIST / max_exact)
                         * (REL_BUCKETS - max_exact)).astype(jnp.int32)
    large = jnp.minimum(large, REL_BUCKETS - 1)
    return jnp.where(n < max_exact, n, large)


def _toeplitz_bias(rel_bias):
    t = LANES
    r = jnp.arange(t, dtype=jnp.int32)
    tiles = []
    for m in range(3):
        dist = m * t + r[:, None] - r[None, :] if m < 2 else jnp.full((t, t), 2 * REL_MAX_DIST, jnp.int32)
        tiles.append(rel_bias[_rel_bucket(dist)])
    return jnp.transpose(jnp.stack(tiles), (3, 0, 1, 2)).astype(F32)


def _out_proj_kernel(ya_ref, yb_ref, wa_ref, wb_ref, x_ref, o_ref):
    o_ref[...] = x_ref[...] + _dot(ya_ref[...], wa_ref[...]) + _dot(yb_ref[...], wb_ref[...])


def out_proj(ya, yb, w_out, x, tm=512, tn=1024):
    n, ka = ya.shape
    kb = yb.shape[1]
    d = x.shape[1]
    wa, wb = w_out[:ka], w_out[ka:]
    return pl.pallas_call(
        _out_proj_kernel,
        grid=(n // tm, d // tn),
        in_specs=[pl.BlockSpec((tm, ka), lambda i, j: (i, 0)),
                  pl.BlockSpec((tm, kb), lambda i, j: (i, 0)),
                  pl.BlockSpec((ka, tn), lambda i, j: (0, j)),
                  pl.BlockSpec((kb, tn), lambda i, j: (0, j)),
                  pl.BlockSpec((tm, tn), lambda i, j: (i, j))],
        out_specs=pl.BlockSpec((tm, tn), lambda i, j: (i, j)),
        out_shape=jax.ShapeDtypeStruct((n, d), F32),
        compiler_params=_params(("parallel", "parallel"), 40),
        name="out_proj",
    )(ya, yb, wa, wb, x)


def _silu(a):
    return a / (1.0 + jnp.exp(-a))


def _ffn_kernel(x_ref, g_ref, wg_ref, wu_ref, wd_ref, o_ref, h_ref, acc_ref):
    f = pl.program_id(1)

    @pl.when(f == 0)
    def _():
        h_ref[...] = _rms(x_ref[...], g_ref[...]).astype(BF16)
        acc_ref[...] = jnp.zeros(acc_ref.shape, F32)

    h = h_ref[...]
    act = (_silu(_dot(h, wg_ref[...])) * _dot(h, wu_ref[...])).astype(BF16)
    acc_ref[...] += _dot(act, wd_ref[...])

    @pl.when(f == pl.num_programs(1) - 1)
    def _():
        o_ref[...] = x_ref[...] + acc_ref[...]


def ffn(x, g, w_gate, w_up, w_down, tm=512, tf=512):
    n, d = x.shape
    dff = w_gate.shape[1]
    return pl.pallas_call(
        _ffn_kernel,
        grid=(n // tm, dff // tf),
        in_specs=[pl.BlockSpec((tm, d), lambda i, f: (i, 0)),
                  pl.BlockSpec((1, d), lambda i, f: (0, 0)),
                  pl.BlockSpec((d, tf), lambda i, f: (0, f)),
                  pl.BlockSpec((d, tf), lambda i, f: (0, f)),
                  pl.BlockSpec((tf, d), lambda i, f: (f, 0))],
        out_specs=pl.BlockSpec((tm, d), lambda i, f: (i, 0)),
        out_shape=jax.ShapeDtypeStruct((n, d), F32),
        scratch_shapes=[pltpu.VMEM((tm, d), BF16), pltpu.VMEM((tm, d), F32)],
        compiler_params=_params(("parallel", "arbitrary"), 48),
        name="ffn",
    )(x, g.reshape(1, d), w_gate, w_up, w_down)


def _gelu(x):
    return x * (0.5 * (1.0 + jnp.tanh(math.sqrt(2.0 / math.pi) * (x + 0.044715 * (x * x * x)))))


def _gmlp_kernel(cu_ref, cv_ref, gv_ref, ws_ref, bst_ref, o_ref):
    tm = cu_ref.shape[0]
    ng = ws_ref.shape[0]
    gw = cu_ref.shape[1] // ng
    u = _gelu(cu_ref[...].astype(F32))
    vn = _rms(_gelu(cv_ref[...].astype(F32)), gv_ref[...]).astype(BF16)
    row = lax.broadcasted_iota(jnp.int32, (CHUNK, CHUNK), 0)
    col = lax.broadcasted_iota(jnp.int32, (CHUNK, CHUNK), 1)
    for g in range(ng):
        wsg = jnp.where(row >= col, ws_ref[g], 0.0).astype(BF16)
        bias = bst_ref[:, g:g + 1]
        cols = slice(g * gw, (g + 1) * gw)
        for c in range(tm // CHUNK):
            rows = slice(c * CHUNK, (c + 1) * CHUNK)
            mixed = _dot(wsg, vn[rows, cols]) + bias
            o_ref[rows, cols] = (u[rows, cols] * mixed).astype(o_ref.dtype)


def gmlp_mixer(proj, g_v, w_s, b_s_t, c_width, tm=256):
    n = proj.shape[0]
    ng = w_s.shape[0]
    return pl.pallas_call(
        _gmlp_kernel,
        grid=(n // tm,),
        in_specs=[pl.BlockSpec((tm, c_width), lambda i: (i, 0)),
                  pl.BlockSpec((tm, c_width), lambda i: (i, 1)),
                  pl.BlockSpec((1, c_width), lambda i: (0, 0)),
                  pl.BlockSpec((ng, CHUNK, CHUNK), lambda i: (0, 0, 0)),
                  pl.BlockSpec((CHUNK, ng), lambda i: (0, 0))],
        out_specs=pl.BlockSpec((tm, c_width), lambda i: (i, 0)),
        out_shape=jax.ShapeDtypeStruct((n, c_width), BF16),
        compiler_params=_params(("parallel",), 40),
        name="gmlp_mixer",
    )(proj, proj, g_v.reshape(1, -1), w_s, b_s_t)


def _sb_kernel(q_ref, k_ref, v_ref, u_ref, o_ref, acc_ref, carry_ref, *, tk):
    tq = q_ref.shape[1]
    i = pl.program_id(2)
    scale = HEAD_DIM ** -0.5
    acc_ref[...] = jnp.zeros(acc_ref.shape, F32)
    carry_ref[...] = jnp.zeros(carry_ref.shape, F32)
    q = q_ref[0]
    tpos = i * tq + lax.broadcasted_iota(jnp.int32, (tq, tk), 0)
    col = lax.broadcasted_iota(jnp.int32, (tq, tk), 1)
    nkb = (i + 1) * (tq // tk)

    def body(step, c):
        j = nkb - 1 - step
        start = pl.multiple_of(j * tk, tk)
        kj = k_ref[0, pl.ds(start, tk), :]
        vj = v_ref[0, pl.ds(start, tk), :]
        z = _dot_nt(q, kj) * scale
        t = jnp.log1p(jnp.exp(-jnp.abs(z)))
        past = (j * tk + col) < tpos
        log_keep = jnp.where(past, -(jnp.maximum(z, 0.0) + t), 0.0)
        log_beta = jnp.minimum(z, 0.0) - t
        hi = log_keep.astype(BF16)
        lo = (log_keep - hi.astype(F32)).astype(BF16)
        excl = _dot(hi, u_ref[...]) + _dot(lo, u_ref[...])
        wts = jnp.where(past, jnp.exp(log_beta + excl + carry_ref[...]), 0.0)
        acc_ref[...] += _dot(wts.astype(BF16), vj)
        carry_ref[...] += jnp.sum(log_keep, axis=1, keepdims=True)
        return c

    lax.fori_loop(0, nkb, body, 0)
    o_ref[0] = acc_ref[...].astype(o_ref.dtype)


def stick_breaking(proj, nh, q_col, k_col, v_col, tq=512, tk=256):
    bsz, seq, _ = proj.shape
    tq = min(tq, seq)
    tk = min(tk, tq)
    r = jnp.arange(tk, dtype=jnp.int32)
    upper = (r[:, None] > r[None, :]).astype(BF16)
    qb, kb, vb = q_col // HEAD_DIM, k_col // HEAD_DIM, v_col // HEAD_DIM
    return pl.pallas_call(
        functools.partial(_sb_kernel, tk=tk),
        grid=(bsz, nh, seq // tq),
        in_specs=[pl.BlockSpec((1, tq, HEAD_DIM), lambda b, h, i: (b, i, qb + h)),
                  pl.BlockSpec((1, seq, HEAD_DIM), lambda b, h, i: (b, 0, kb + h)),
                  pl.BlockSpec((1, seq, HEAD_DIM), lambda b, h, i: (b, 0, vb + h)),
                  pl.BlockSpec((tk, tk), lambda b, h, i: (0, 0))],
        out_specs=pl.BlockSpec((1, tq, HEAD_DIM), lambda b, h, i: (b, i, h)),
        out_shape=jax.ShapeDtypeStruct((bsz, seq, nh * HEAD_DIM), BF16),
        scratch_shapes=[pltpu.VMEM((tq, HEAD_DIM), F32), pltpu.VMEM((tq, 1), F32)],
        compiler_params=_params(("parallel", "parallel", "arbitrary"), 40),
        name="stick_breaking",
    )(proj, proj, proj, upper)


def _moe_kernel(x_ref, g_ref, rhi_ref, rlo_ref, wg_ref, wu_ref, wd_ref, gf_ref, o_ref,
                h_ref, acc_ref, comb_ref, *, n_exp):
    e = pl.program_id(1)
    f = pl.program_id(2)
    tm = x_ref.shape[0]
    lane = lax.broadcasted_iota(jnp.int32, (tm, LANES), 1)

    @pl.when((e == 0) & (f == 0))
    def _():
        h32 = _rms(x_ref[...], g_ref[...])
        hi = h32.astype(BF16)
        lo = (h32 - hi.astype(F32)).astype(BF16)
        h_ref[...] = hi
        acc_ref[...] = jnp.zeros(acc_ref.shape, F32)
        logits = _dot(hi, rhi_ref[...]) + (_dot(hi, rlo_ref[...]) + _dot(lo, rhi_ref[...]))
        lanef = lane.astype(F32)
        lg = jnp.where(lane < n_exp, logits, -jnp.inf)
        m1 = jnp.max(lg, axis=1, keepdims=True)
        i1 = jnp.min(jnp.where(lg == m1, lanef, float(LANES)), axis=1, keepdims=True)
        lg2 = jnp.where(lanef == i1, -jnp.inf, lg)
        m2 = jnp.max(lg2, axis=1, keepdims=True)
        i2 = jnp.min(jnp.where(lg2 == m2, lanef, float(LANES)), axis=1, keepdims=True)
        e2 = jnp.exp(m2 - m1)
        g1 = 1.0 / (1.0 + e2)
        comb_ref[...] = jnp.where(lanef == i1, g1, 0.0) + jnp.where(lanef == i2, e2 * g1, 0.0)

    gate = jnp.sum(jnp.where(lane == e, comb_ref[...], 0.0), axis=1, keepdims=True)
    h = h_ref[...]
    act = (_silu(_dot(h, wg_ref[0])) * _dot(h, wu_ref[0]) * gate).astype(BF16)
    acc_ref[...] += _dot(act, wd_ref[0])

    @pl.when((e == pl.num_programs(1) - 1) & (f == pl.num_programs(2) - 1))
    def _():
        o_ref[...] = _rms(x_ref[...] + acc_ref[...], gf_ref[...])


def moe(x, g, r_hi, r_lo, w_gate, w_up, w_down, g_final, tm=512, tf=512):
    n, d = x.shape
    n_exp, _, dff = w_gate.shape
    return pl.pallas_call(
        functools.partial(_moe_kernel, n_exp=n_exp),
        grid=(n // tm, n_exp, dff // tf),
        in_specs=[pl.BlockSpec((tm, d), lambda i, e, f: (i, 0)),
                  pl.BlockSpec((1, d), lambda i, e, f: (0, 0)),
                  pl.BlockSpec((d, LANES), lambda i, e, f: (0, 0)),
                  pl.BlockSpec((d, LANES), lambda i, e, f: (0, 0)),
                  pl.BlockSpec((1, d, tf), lambda i, e, f: (e, 0, f)),
                  pl.BlockSpec((1, d, tf), lambda i, e, f: (e, 0, f)),
                  pl.BlockSpec((1, tf, d), lambda i, e, f: (e, f, 0)),
                  pl.BlockSpec((1, d), lambda i, e, f: (0, 0))],
        out_specs=pl.BlockSpec((tm, d), lambda i, e, f: (i, 0)),
        out_shape=jax.ShapeDtypeStruct((n, d), F32),
        scratch_shapes=[pltpu.VMEM((tm, d), BF16), pltpu.VMEM((tm, d), F32), pltpu.VMEM((tm, LANES), F32)],
        compiler_params=_params(("parallel", "arbitrary", "arbitrary"), 48),
        name="moe",
    )(x, g.reshape(1, d), r_hi, r_lo, w_gate, w_up, w_down, g_final.reshape(1, d))


def _pad_cols(w, width):
    return jnp.pad(w, ((0, 0), (0, width - w.shape[1])))


def _even_layer(x2d, bsz, seq, rel_bias, norm_mix, w_in, conv_w, norm_cq, norm_ckv, w_uq, w_uk, w_uv, w_qidx,
                w_out, norm_ffn, ffn_gate, ffn_up, ffn_down):
    nh = w_uq.shape[1]
    a_width = conv_w.shape[1]
    c0 = 3 * a_width
    cq_col, ckv_col = c0, c0 + Q_RANK
    kidx_col = ckv_col + KV_RANK
    widx_col = kidx_col + LANES
    w_in_p = jnp.concatenate([
        w_in[:, :kidx_col],
        _pad_cols(w_in[:, kidx_col:kidx_col + IDX_DIM], LANES),
        _pad_cols(w_in[:, kidx_col + IDX_DIM:], LANES)], axis=1).astype(BF16)
    proj = norm_matmul(x2d, norm_mix, w_in_p).reshape(bsz, seq, -1)

    y_a = conv_mixer(proj, conv_w, a_width)

    w_ukt = jnp.transpose(w_uk, (1, 2, 0)).astype(BF16)
    w_qidx_pad = jnp.pad(w_qidx, ((0, 0), (0, 0), (0, LANES - IDX_DIM))).reshape(Q_RANK, IDX_HEADS * LANES)
    qlat, qidx, ckv = dsa_prep(proj, norm_cq, norm_ckv, w_uq.reshape(Q_RANK, nh * HEAD_DIM).astype(BF16),
                               w_ukt, w_qidx_pad.astype(BF16), cq_col, ckv_col)
    y_b = dsa_attention(qlat, qidx, proj, ckv, _toeplitz_bias(rel_bias),
                        jnp.transpose(w_uv, (1, 0, 2)).astype(BF16), kidx_col, widx_col,
                        topk=min(TOPK_MAX, seq // 4))

    n = bsz * seq
    x2d = out_proj(y_a.reshape(n, -1), y_b.reshape(n, -1), w_out.astype(BF16), x2d)
    return ffn(x2d, norm_ffn, ffn_gate.astype(BF16), ffn_up.astype(BF16), ffn_down.astype(BF16))


def _odd_layer(x2d, bsz, seq, norm_mix, w_in, norm_v, w_s, b_s, w_out, norm_ffn, router, exp_gate, exp_up,
               exp_down, final_norm):
    c_width = norm_v.shape[0]
    nh = (w_in.shape[1] - 2 * c_width) // (3 * HEAD_DIM)
    d_width = nh * HEAD_DIM
    proj = norm_matmul(x2d, norm_mix, w_in.astype(BF16))
    y_c = gmlp_mixer(proj, norm_v, w_s, jnp.transpose(b_s), c_width)
    y_d = stick_breaking(proj.reshape(bsz, seq, -1), nh, 2 * c_width, 2 * c_width + d_width,
                         2 * c_width + 2 * d_width)
    x2d = out_proj(y_c, y_d.reshape(bsz * seq, -1), w_out.astype(BF16), x2d)
    r_pad = _pad_cols(router, LANES)
    r_hi = r_pad.astype(BF16)
    r_lo = (r_pad - r_hi.astype(F32)).astype(BF16)
    return moe(x2d, norm_ffn, r_hi, r_lo, exp_gate.astype(BF16), exp_up.astype(BF16), exp_down.astype(BF16),
               final_norm)


def kernel(x, rel_bias, final_norm, e_norm_mix, e_w_in, e_conv_w, e_norm_cq, e_norm_ckv, e_w_uq, e_w_uk, e_w_uv, e_w_qidx, e_w_out, e_norm_ffn, e_ffn_gate, e_ffn_up, e_ffn_down, o_norm_mix, o_w_in, o_norm_v, o_w_s, o_b_s, o_w_out, o_norm_ffn, o_router, o_exp_gate, o_exp_up, o_exp_down):
    bsz, seq, d = x.shape
    assert e_norm_mix.shape[0] == 1 and o_norm_mix.shape[0] == 1, "one even and one odd layer"
    x2d = x.reshape(bsz * seq, d)
    x2d = _even_layer(x2d, bsz, seq, rel_bias, e_norm_mix[0], e_w_in[0], e_conv_w[0], e_norm_cq[0],
                      e_norm_ckv[0], e_w_uq[0], e_w_uk[0], e_w_uv[0], e_w_qidx[0], e_w_out[0], e_norm_ffn[0],
                      e_ffn_gate[0], e_ffn_up[0], e_ffn_down[0])
    out = _odd_layer(x2d, bsz, seq, o_norm_mix[0], o_w_in[0], o_norm_v[0], o_w_s[0], o_b_s[0], o_w_out[0],
                     o_norm_ffn[0], o_router[0], o_exp_gate[0], o_exp_up[0], o_exp_down[0], final_norm)
    return out.reshape(bsz, seq, d)
```

```python
import functools
import math

import jax
import jax.numpy as jnp
from jax import lax
from jax.experimental import pallas as pl
from jax.experimental.pallas import tpu as pltpu

F32 = jnp.float32
BF16 = jnp.bfloat16

EPS = 1e-6
LANES = 128
HEAD_DIM = 128
Q_RANK = 512
KV_RANK = 256
IDX_HEADS = 16
IDX_DIM = 64
IDX_SCALE = (IDX_DIM ** -0.5) * (IDX_HEADS ** -0.5)
TOPK_MAX = 256
REL_BUCKETS = 32
REL_MAX_DIST = 128
CONV_WIDTH = 3
CHUNK = 128
TOP_K = 2
INT_MIN = -(2 ** 31)
MIB = 1024 * 1024


def _params(sem, vmem_mib):
    return pltpu.CompilerParams(dimension_semantics=sem, vmem_limit_bytes=vmem_mib * MIB)


def _rms(x, g):
    return x * lax.rsqrt(jnp.mean(x * x, axis=-1, keepdims=True) + EPS) * g


def _dot(a, b):
    return jnp.dot(a, b, preferred_element_type=F32)


def _dot_nt(a, b):
    return lax.dot_general(a, b, (((1,), (1,)), ((), ())), preferred_element_type=F32)


def _norm_matmul_kernel(x_ref, g_ref, w_ref, o_ref, h_ref):
    @pl.when(pl.program_id(1) == 0)
    def _():
        h_ref[...] = _rms(x_ref[...], g_ref[...]).astype(BF16)

    o_ref[...] = _dot(h_ref[...], w_ref[...]).astype(o_ref.dtype)


def norm_matmul(x, g, w, tm=512, tn=1024):
    n, d = x.shape
    nout = w.shape[1]
    return pl.pallas_call(
        _norm_matmul_kernel,
        grid=(n // tm, nout // tn),
        in_specs=[pl.BlockSpec((tm, d), lambda i, j: (i, 0)),
                  pl.BlockSpec((1, d), lambda i, j: (0, 0)),
                  pl.BlockSpec((d, tn), lambda i, j: (0, j))],
        out_specs=pl.BlockSpec((tm, tn), lambda i, j: (i, j)),
        out_shape=jax.ShapeDtypeStruct((n, nout), BF16),
        scratch_shapes=[pltpu.VMEM((tm, d), BF16)],
        compiler_params=_params(("parallel", "arbitrary"), 40),
        name="norm_matmul",
    )(x, g.reshape(1, d), w)


def _conv_kernel(b_ref, c_ref, x_ref, w_ref, o_ref):
    z = c_ref[0].astype(F32) * x_ref[0].astype(F32)
    row = lax.broadcasted_iota(jnp.int32, z.shape, 0)
    y = w_ref[CONV_WIDTH - 1:CONV_WIDTH, :] * z
    for lag in range(1, CONV_WIDTH):
        zl = jnp.where(row >= lag, pltpu.roll(z, lag, 0), 0.0)
        y = y + w_ref[CONV_WIDTH - 1 - lag:CONV_WIDTH - lag, :] * zl
    o_ref[0] = (b_ref[0].astype(F32) * y).astype(o_ref.dtype)


def conv_mixer(proj, conv_w, a_width, tc=256):
    bsz, seq, _ = proj.shape
    nb = a_width // tc
    return pl.pallas_call(
        _conv_kernel,
        grid=(bsz, nb),
        in_specs=[pl.BlockSpec((1, seq, tc), lambda b, c: (b, 0, c)),
                  pl.BlockSpec((1, seq, tc), lambda b, c: (b, 0, nb + c)),
                  pl.BlockSpec((1, seq, tc), lambda b, c: (b, 0, 2 * nb + c)),
                  pl.BlockSpec((CONV_WIDTH, tc), lambda b, c: (0, c))],
        out_specs=pl.BlockSpec((1, seq, tc), lambda b, c: (b, 0, c)),
        out_shape=jax.ShapeDtypeStruct((bsz, seq, a_width), BF16),
        compiler_params=_params(("parallel", "parallel"), 40),
        name="conv_mixer",
    )(proj, proj, proj, conv_w)


def _dsa_prep_kernel(cq_ref, ckv_ref, w_ref, gq_ref, gkv_ref, wuq_ref, wukt_ref, wqi_ref,
                     qlat_ref, qidx_ref, ckvo_ref, ckvt_ref, wt_ref):
    tm = cq_ref.shape[1]
    cqn = _rms(cq_ref[0].astype(F32), gq_ref[...]).astype(BF16)
    ckvn = _rms(ckv_ref[0].astype(F32), gkv_ref[...])
    ckvo_ref[0] = ckvn.astype(BF16)
    for c in range(tm // LANES):
        ckvt_ref[0, c] = jnp.transpose(ckvn[c * LANES:(c + 1) * LANES, :]).astype(BF16)
    wt_ref[0] = jnp.transpose(w_ref[0].astype(F32))[:IDX_HEADS, :] * IDX_SCALE
    q = _dot(cqn, wuq_ref[...]).astype(BF16)
    scale = HEAD_DIM ** -0.5
    for h in range(wukt_ref.shape[0]):
        qh = q[:, h * HEAD_DIM:(h + 1) * HEAD_DIM]
        qlat_ref[0, h] = (_dot(qh, wukt_ref[h]) * scale).astype(BF16)
    qi = _dot(cqn, wqi_ref[...]).astype(BF16)
    for g in range(IDX_HEADS):
        qidx_ref[0, g] = qi[:, g * LANES:(g + 1) * LANES]


def dsa_prep(proj, g_cq, g_ckv, w_uq, w_ukt, w_qidx_pad, cq_col, ckv_col, widx_col, tm=256):
    bsz, seq, _ = proj.shape
    nh = w_ukt.shape[0]
    full = lambda *shape: pl.BlockSpec(shape, lambda b, i: (0,) * len(shape))
    return pl.pallas_call(
        _dsa_prep_kernel,
        grid=(bsz, seq // tm),
        in_specs=[pl.BlockSpec((1, tm, Q_RANK), lambda b, i: (b, i, cq_col // Q_RANK)),
                  pl.BlockSpec((1, tm, KV_RANK), lambda b, i: (b, i, ckv_col // KV_RANK)),
                  pl.BlockSpec((1, tm, LANES), lambda b, i: (b, i, widx_col // LANES)),
                  full(1, Q_RANK), full(1, KV_RANK),
                  full(Q_RANK, nh * HEAD_DIM), full(nh, HEAD_DIM, KV_RANK),
                  full(Q_RANK, IDX_HEADS * LANES)],
        out_specs=[pl.BlockSpec((1, nh, tm, KV_RANK), lambda b, i: (b, 0, i, 0)),
                   pl.BlockSpec((1, IDX_HEADS, tm, LANES), lambda b, i: (b, 0, i, 0)),
                   pl.BlockSpec((1, tm, KV_RANK), lambda b, i: (b, i, 0)),
                   pl.BlockSpec((1, tm // LANES, KV_RANK, LANES), lambda b, i: (b, i, 0, 0)),
                   pl.BlockSpec((1, IDX_HEADS, tm), lambda b, i: (b, 0, i))],
        out_shape=[jax.ShapeDtypeStruct((bsz, nh, seq, KV_RANK), BF16),
                   jax.ShapeDtypeStruct((bsz, IDX_HEADS, seq, LANES), BF16),
                   jax.ShapeDtypeStruct((bsz, seq, KV_RANK), BF16),
                   jax.ShapeDtypeStruct((bsz, seq // LANES, KV_RANK, LANES), BF16),
                   jax.ShapeDtypeStruct((bsz, IDX_HEADS, seq), F32)],
        compiler_params=_params(("parallel", "parallel"), 40),
        name="dsa_prep",
    )(proj, proj, proj, g_cq.reshape(1, -1), g_ckv.reshape(1, -1), w_uq, w_ukt, w_qidx_pad)


def _dsa_kernel(qlat_ref, qidx_ref, wt_ref, kidx_ref, ckv_ref, ckvt_ref, toep_ref, wuvt_ref, o_ref,
                sc_ref, key_ref, p_ref, m_ref, l_ref, acc_ref, *, topk):
    tq = LANES
    nh = qlat_ref.shape[1]
    i = pl.program_id(1)
    kpos = lax.broadcasted_iota(jnp.int32, (tq, tq), 0)
    qpos = lax.broadcasted_iota(jnp.int32, (tq, tq), 1)
    wt = wt_ref[0]
    qi = qidx_ref[0].reshape(IDX_HEADS * tq, LANES)

    def score_body(j, carry):
        kj = kidx_ref[0, pl.ds(pl.multiple_of(j * tq, tq), tq), :]
        sg = _dot_nt(kj, qi)
        s = jnp.zeros((tq, tq), F32)
        for g in range(IDX_HEADS):
            s = s + wt[g:g + 1, :] * jnp.maximum(sg[:, g * tq:(g + 1) * tq], 0.0)
        sc_ref[j] = s
        bits = pltpu.bitcast(s, jnp.int32)
        key = bits ^ ((bits >> 31) & 0x7FFFFFFF)
        causal = (j * tq + kpos) <= (i * tq + qpos)
        key_ref[j] = jnp.where(causal, key, INT_MIN)
        return carry

    lax.fori_loop(0, i + 1, score_body, 0)

    kvec = jnp.minimum(topk, i * tq + qpos[:1, :] + 1).astype(F32)

    def count_ge(cand):
        def body(j, acc):
            return acc + jnp.where(key_ref[j] >= cand, 1.0, 0.0)
        acc = lax.fori_loop(0, i + 1, body, jnp.zeros((tq, tq), F32))
        return jnp.sum(acc, axis=0, keepdims=True)

    thr = jnp.where(count_ge(jnp.zeros((1, tq), jnp.int32)) >= kvec, 0, INT_MIN).astype(jnp.int32)

    def bit_body(b, thr):
        cand = thr | jnp.left_shift(jnp.int32(1), 30 - b)
        return jnp.where(count_ge(cand) >= kvec, cand, thr)

    thr = lax.fori_loop(0, 31, bit_body, thr)

    m_ref[...] = jnp.full(m_ref.shape, -jnp.inf, F32)
    l_ref[...] = jnp.zeros(l_ref.shape, F32)
    acc_ref[...] = jnp.zeros(acc_ref.shape, F32)
    q = qlat_ref[0].reshape(nh * tq, KV_RANK)

    def att_body(j, carry):
        cj = ckv_ref[0, pl.ds(pl.multiple_of(j * tq, tq), tq), :]
        lg = _dot_nt(cj, q)
        sel = key_ref[j] >= thr
        sj = sc_ref[j]
        mdist = jnp.minimum(i - j, 2)
        for h in range(nh):
            cols = slice(h * tq, (h + 1) * tq)
            lh = jnp.where(sel, lg[:, cols] + toep_ref[h, mdist] + sj, -jnp.inf)
            m_old = m_ref[:, cols]
            m_new = jnp.maximum(m_old, jnp.max(lh, axis=0, keepdims=True))
            m_safe = jnp.where(m_new == -jnp.inf, 0.0, m_new)
            alpha = jnp.exp(m_old - m_safe)
            p = jnp.exp(lh - m_safe)
            m_ref[:, cols] = m_new
            l_ref[:, cols] = alpha * l_ref[:, cols] + jnp.sum(p, axis=0, keepdims=True)
            acc_ref[:, cols] = alpha * acc_ref[:, cols]
            p_ref[:, cols] = p.astype(BF16)
        acc_ref[...] += _dot(ckvt_ref[0, j], p_ref[...])
        return carry

    lax.fori_loop(0, i + 1, att_body, 0)

    for h in range(nh):
        cols = slice(h * tq, (h + 1) * tq)
        o_lat_t = (acc_ref[:, cols] / l_ref[:, cols]).astype(BF16)
        y_t = _dot(wuvt_ref[h], o_lat_t)
        o_ref[0, :, h * HEAD_DIM:(h + 1) * HEAD_DIM] = jnp.transpose(y_t).astype(o_ref.dtype)


def dsa_attention(qlat, qidx, wt, proj, ckv, ckvt, toep, w_uvt, kidx_col, topk):
    bsz, nh, seq, _ = qlat.shape
    tq = LANES
    nkb = seq // tq
    full = lambda *shape: pl.BlockSpec(shape, lambda b, i: (0,) * len(shape))
    return pl.pallas_call(
        functools.partial(_dsa_kernel, topk=topk),
        grid=(bsz, seq // tq),
        in_specs=[pl.BlockSpec((1, nh, tq, KV_RANK), lambda b, i: (b, 0, i, 0)),
                  pl.BlockSpec((1, IDX_HEADS, tq, LANES), lambda b, i: (b, 0, i, 0)),
                  pl.BlockSpec((1, IDX_HEADS, tq), lambda b, i: (b, 0, i)),
                  pl.BlockSpec((1, seq, LANES), lambda b, i: (b, 0, kidx_col // LANES)),
                  pl.BlockSpec((1, seq, KV_RANK), lambda b, i: (b, 0, 0)),
                  pl.BlockSpec((1, nkb, KV_RANK, tq), lambda b, i: (b, 0, 0, 0)),
                  full(nh, 3, tq, tq),
                  full(nh, HEAD_DIM, KV_RANK)],
        out_specs=pl.BlockSpec((1, tq, nh * HEAD_DIM), lambda b, i: (b, i, 0)),
        out_shape=jax.ShapeDtypeStruct((bsz, seq, nh * HEAD_DIM), BF16),
        scratch_shapes=[pltpu.VMEM((nkb, tq, tq), F32),
                        pltpu.VMEM((nkb, tq, tq), jnp.int32),
                        pltpu.VMEM((tq, nh * tq), BF16),
                        pltpu.VMEM((1, nh * tq), F32),
                        pltpu.VMEM((1, nh * tq), F32),
                        pltpu.VMEM((KV_RANK, nh * tq), F32)],
        compiler_params=_params(("parallel", "arbitrary"), 40),
        name="dsa_attention",
    )(qlat, qidx, wt, proj, ckv, ckvt, toep, w_uvt)


def _rel_bucket(dist):
    n = jnp.maximum(dist, 0)
    max_exact = REL_BUCKETS // 2
    nf = jnp.maximum(n, max_exact).astype(F32)
    large = max_exact + (jnp.log(nf / max_exact) / math.log(REL_MAX_DIST / max_exact)
                         * (REL_BUCKETS - max_exact)).astype(jnp.int32)
    large = jnp.minimum(large, REL_BUCKETS - 1)
    return jnp.where(n < max_exact, n, large)


def _toeplitz_bias(rel_bias):
    t = LANES
    r = jnp.arange(t, dtype=jnp.int32)
    tiles = []
    for m in range(3):
        dist = m * t + r[None, :] - r[:, None] if m < 2 else jnp.full((t, t), 2 * REL_MAX_DIST, jnp.int32)
        tiles.append(rel_bias[_rel_bucket(dist)])
    return jnp.transpose(jnp.stack(tiles), (3, 0, 1, 2)).astype(F32)


def _out_proj_kernel(ya_ref, yb_ref, wa_ref, wb_ref, x_ref, o_ref):
    o_ref[...] = x_ref[...] + _dot(ya_ref[...], wa_ref[...]) + _dot(yb_ref[...], wb_ref[...])


def out_proj(ya, yb, w_out, x, tm=512, tn=1024):
    n, ka = ya.shape
    kb = yb.shape[1]
    d = x.shape[1]
    wa, wb = w_out[:ka], w_out[ka:]
    return pl.pallas_call(
        _out_proj_kernel,
        grid=(n // tm, d // tn),
        in_specs=[pl.BlockSpec((tm, ka), lambda i, j: (i, 0)),
                  pl.BlockSpec((tm, kb), lambda i, j: (i, 0)),
                  pl.BlockSpec((ka, tn), lambda i, j: (0, j)),
                  pl.BlockSpec((kb, tn), lambda i, j: (0, j)),
                  pl.BlockSpec((tm, tn), lambda i, j: (i, j))],
        out_specs=pl.BlockSpec((tm, tn), lambda i, j: (i, j)),
        out_shape=jax.ShapeDtypeStruct((n, d), F32),
        compiler_params=_params(("parallel", "parallel"), 40),
        name="out_proj",
    )(ya, yb, wa, wb, x)


def _silu(a):
    return a / (1.0 + jnp.exp(-a))


def _ffn_kernel(x_ref, g_ref, wg_ref, wu_ref, wd_ref, o_ref, h_ref, acc_ref):
    f = pl.program_id(1)

    @pl.when(f == 0)
    def _():
        h_ref[...] = _rms(x_ref[...], g_ref[...]).astype(BF16)
        acc_ref[...] = jnp.zeros(acc_ref.shape, F32)

    h = h_ref[...]
    act = (_silu(_dot(h, wg_ref[...])) * _dot(h, wu_ref[...])).astype(BF16)
    acc_ref[...] += _dot(act, wd_ref[...])

    @pl.when(f == pl.num_programs(1) - 1)
    def _():
        o_ref[...] = x_ref[...] + acc_ref[...]


def ffn(x, g, w_gate, w_up, w_down, tm=512, tf=512):
    n, d = x.shape
    dff = w_gate.shape[1]
    return pl.pallas_call(
        _ffn_kernel,
        grid=(n // tm, dff // tf),
        in_specs=[pl.BlockSpec((tm, d), lambda i, f: (i, 0)),
                  pl.BlockSpec((1, d), lambda i, f: (0, 0)),
                  pl.BlockSpec((d, tf), lambda i, f: (0, f)),
                  pl.BlockSpec((d, tf), lambda i, f: (0, f)),
                  pl.BlockSpec((tf, d), lambda i, f: (f, 0))],
        out_specs=pl.BlockSpec((tm, d), lambda i, f: (i, 0)),
        out_shape=jax.ShapeDtypeStruct((n, d), F32),
        scratch_shapes=[pltpu.VMEM((tm, d), BF16), pltpu.VMEM((tm, d), F32)],
        compiler_params=_params(("parallel", "arbitrary"), 48),
        name="ffn",
    )(x, g.reshape(1, d), w_gate, w_up, w_down)


def _gelu(x):
    return x * (0.5 * (1.0 + jnp.tanh(math.sqrt(2.0 / math.pi) * (x + 0.044715 * (x * x * x)))))


def _gmlp_kernel(cu_ref, cv_ref, gv_ref, ws_ref, bst_ref, o_ref):
    tm = cu_ref.shape[0]
    ng = ws_ref.shape[0]
    gw = cu_ref.shape[1] // ng
    u = _gelu(cu_ref[...].astype(F32))
    vn = _rms(_gelu(cv_ref[...].astype(F32)), gv_ref[...]).astype(BF16)
    row = lax.broadcasted_iota(jnp.int32, (CHUNK, CHUNK), 0)
    col = lax.broadcasted_iota(jnp.int32, (CHUNK, CHUNK), 1)
    for g in range(ng):
        wsg = jnp.where(row >= col, ws_ref[g], 0.0).astype(BF16)
        bias = bst_ref[:, g:g + 1]
        cols = slice(g * gw, (g + 1) * gw)
        for c in range(tm // CHUNK):
            rows = slice(c * CHUNK, (c + 1) * CHUNK)
            mixed = _dot(wsg, vn[rows, cols]) + bias
            o_ref[rows, cols] = (u[rows, cols] * mixed).astype(o_ref.dtype)


def gmlp_mixer(proj, g_v, w_s, b_s_t, c_width, tm=256):
    n = proj.shape[0]
    ng = w_s.shape[0]
    return pl.pallas_call(
        _gmlp_kernel,
        grid=(n // tm,),
        in_specs=[pl.BlockSpec((tm, c_width), lambda i: (i, 0)),
                  pl.BlockSpec((tm, c_width), lambda i: (i, 1)),
                  pl.BlockSpec((1, c_width), lambda i: (0, 0)),
                  pl.BlockSpec((ng, CHUNK, CHUNK), lambda i: (0, 0, 0)),
                  pl.BlockSpec((CHUNK, ng), lambda i: (0, 0))],
        out_specs=pl.BlockSpec((tm, c_width), lambda i: (i, 0)),
        out_shape=jax.ShapeDtypeStruct((n, c_width), BF16),
        compiler_params=_params(("parallel",), 40),
        name="gmlp_mixer",
    )(proj, proj, g_v.reshape(1, -1), w_s, b_s_t)


def _sb_kernel(q_ref, k_ref, v_ref, u_ref, o_ref, acc_ref, carry_ref, *, tk):
    tq = q_ref.shape[1]
    i = pl.program_id(2)
    scale = HEAD_DIM ** -0.5
    acc_ref[...] = jnp.zeros(acc_ref.shape, F32)
    carry_ref[...] = jnp.zeros(carry_ref.shape, F32)
    q = q_ref[0]
    tpos = i * tq + lax.broadcasted_iota(jnp.int32, (tq, tk), 0)
    col = lax.broadcasted_iota(jnp.int32, (tq, tk), 1)
    nkb = (i + 1) * (tq // tk)

    def body(step, c):
        j = nkb - 1 - step
        start = pl.multiple_of(j * tk, tk)
        kj = k_ref[0, pl.ds(start, tk), :]
        vj = v_ref[0, pl.ds(start, tk), :]
        z = _dot_nt(q, kj) * scale
        t = jnp.log1p(jnp.exp(-jnp.abs(z)))
        past = (j * tk + col) < tpos
        log_keep = jnp.where(past, -(jnp.maximum(z, 0.0) + t), 0.0)
        log_beta = jnp.minimum(z, 0.0) - t
        hi = log_keep.astype(BF16)
        lo = (log_keep - hi.astype(F32)).astype(BF16)
        excl = _dot(hi, u_ref[...]) + _dot(lo, u_ref[...])
        wts = jnp.where(past, jnp.exp(log_beta + excl + carry_ref[...]), 0.0)
        acc_ref[...] += _dot(wts.astype(BF16), vj)
        carry_ref[...] += jnp.sum(log_keep, axis=1, keepdims=True)
        return c

    lax.fori_loop(0, nkb, body, 0)
    o_ref[0] = acc_ref[...].astype(o_ref.dtype)


def stick_breaking(proj, nh, q_col, k_col, v_col, tq=512, tk=256):
    bsz, seq, _ = proj.shape
    tq = min(tq, seq)
    tk = min(tk, tq)
    r = jnp.arange(tk, dtype=jnp.int32)
    upper = (r[:, None] > r[None, :]).astype(BF16)
    qb, kb, vb = q_col // HEAD_DIM, k_col // HEAD_DIM, v_col // HEAD_DIM
    return pl.pallas_call(
        functools.partial(_sb_kernel, tk=tk),
        grid=(bsz, nh, seq // tq),
        in_specs=[pl.BlockSpec((1, tq, HEAD_DIM), lambda b, h, i: (b, i, qb + h)),
                  pl.BlockSpec((1, seq, HEAD_DIM), lambda b, h, i: (b, 0, kb + h)),
                  pl.BlockSpec((1, seq, HEAD_DIM), lambda b, h, i: (b, 0, vb + h)),
                  pl.BlockSpec((tk, tk), lambda b, h, i: (0, 0))],
        out_specs=pl.BlockSpec((1, tq, HEAD_DIM), lambda b, h, i: (b, i, h)),
        out_shape=jax.ShapeDtypeStruct((bsz, seq, nh * HEAD_DIM), BF16),
        scratch_shapes=[pltpu.VMEM((tq, HEAD_DIM), F32), pltpu.VMEM((tq, 1), F32)],
        compiler_params=_params(("parallel", "parallel", "arbitrary"), 40),
        name="stick_breaking",
    )(proj, proj, proj, upper)


MOE_CHUNK = 128
MOE_SEG_ALIGN = 16
MOE_ROW_BLOCK = 1024


def _moe_route_kernel(x_ref, g_ref, rhi_ref, rlo_ref, h_ref, comb_ref, cnt_ref, *, n_exp):
    tm = x_ref.shape[0]
    h32 = _rms(x_ref[...], g_ref[...])
    hi = h32.astype(BF16)
    lo = (h32 - hi.astype(F32)).astype(BF16)
    h_ref[...] = hi
    logits = _dot(hi, rhi_ref[...]) + (_dot(hi, rlo_ref[...]) + _dot(lo, rhi_ref[...]))
    lane = lax.broadcasted_iota(jnp.int32, (tm, LANES), 1)
    lanef = lane.astype(F32)
    lg = jnp.where(lane < n_exp, logits, -jnp.inf)
    m1 = jnp.max(lg, axis=1, keepdims=True)
    i1 = jnp.min(jnp.where(lg == m1, lanef, float(LANES)), axis=1, keepdims=True)
    lg2 = jnp.where(lanef == i1, -jnp.inf, lg)
    m2 = jnp.max(lg2, axis=1, keepdims=True)
    i2 = jnp.min(jnp.where(lg2 == m2, lanef, float(LANES)), axis=1, keepdims=True)
    e2 = jnp.exp(m2 - m1)
    g1 = 1.0 / (1.0 + e2)
    comb = jnp.where(lanef == i1, g1, 0.0) + jnp.where(lanef == i2, e2 * g1, 0.0)
    comb_ref[...] = comb
    cnt_ref[0] = jnp.sum(jnp.where(comb > 0.0, 1.0, 0.0), axis=0, keepdims=True)


def moe_route(x, g, r_hi, r_lo, n_exp, tm):
    n, d = x.shape
    nt = n // tm
    return pl.pallas_call(
        functools.partial(_moe_route_kernel, n_exp=n_exp),
        grid=(nt,),
        in_specs=[pl.BlockSpec((tm, d), lambda i: (i, 0)),
                  pl.BlockSpec((1, d), lambda i: (0, 0)),
                  pl.BlockSpec((d, LANES), lambda i: (0, 0)),
                  pl.BlockSpec((d, LANES), lambda i: (0, 0))],
        out_specs=[pl.BlockSpec((tm, d), lambda i: (i, 0)),
                   pl.BlockSpec((tm, LANES), lambda i: (i, 0)),
                   pl.BlockSpec((1, 1, LANES), lambda i: (i, 0, 0))],
        out_shape=[jax.ShapeDtypeStruct((n, d), BF16),
                   jax.ShapeDtypeStruct((n, LANES), F32),
                   jax.ShapeDtypeStruct((nt, 1, LANES), F32)],
        compiler_params=_params(("parallel",), 40),
        name="moe_route",
    )(x, g.reshape(1, d), r_hi, r_lo)


def _moe_plan(cnt, n_rows_static, max_pieces):
    nt, n_exp = cnt.shape
    i32 = jnp.int32
    cp = (cnt + MOE_SEG_ALIGN - 1) // MOE_SEG_ALIGN * MOE_SEG_ALIGN
    reg = (jnp.sum(cp, axis=0) + MOE_CHUNK + MOE_ROW_BLOCK - 1) // MOE_ROW_BLOCK * MOE_ROW_BLOCK
    base = jnp.cumsum(reg) - reg
    seg = base[None, :] + jnp.cumsum(cp, axis=0) - cp
    nce = (cnt + MOE_CHUNK - 1) // MOE_CHUNK
    cum = jnp.cumsum(nce, axis=1)
    n_pieces = cum[:, -1]
    k = jnp.arange(max_pieces, dtype=i32)
    ek = jnp.minimum(jnp.sum(cum[:, None, :] <= k[None, :, None], axis=-1), n_exp - 1).astype(i32)
    rk = k[None, :] - jnp.take_along_axis(cum - nce, ek, axis=1)
    off = jnp.take_along_axis(seg, ek, axis=1) + rk * MOE_CHUNK
    valid = k[None, :] < n_pieces[:, None]
    rk = jnp.where(valid, rk, 0).astype(i32)
    off = jnp.where(valid, off, 0).astype(i32)
    n_blocks = n_rows_static // MOE_ROW_BLOCK
    cb = jnp.cumsum(reg // MOE_ROW_BLOCK)
    blk_exp = jnp.minimum(jnp.sum(cb[None, :] <= jnp.arange(n_blocks, dtype=i32)[:, None], axis=1), n_exp - 1)
    return (ek.reshape(-1), rk.reshape(-1), off.reshape(-1), n_pieces.astype(i32),
            blk_exp.astype(i32), cb[-1:].astype(i32))


def _moe_compact_kernel(ek_ref, rk_ref, off_ref, np_ref, h_ref, comb_ref, a_in, g_in, a_out, g_out,
                        pos_ref, p_ref, stage_ref, gst_ref, sem, *, max_pieces):
    del a_in, g_in
    i = pl.program_id(0)
    tm = h_ref.shape[0]
    n_pieces = np_ref[i]
    comb = comb_ref[...]
    mask_t = jnp.transpose(jnp.where(comb > 0.0, 1.0, 0.0))
    r0 = lax.broadcasted_iota(jnp.int32, (tm, tm), 0)
    r1 = lax.broadcasted_iota(jnp.int32, (tm, tm), 1)
    before = jnp.where(r0 < r1, 1.0, 0.0).astype(BF16)
    pos_t = _dot(mask_t.astype(BF16), before)
    pos_ref[...] = jnp.where(mask_t > 0.0, pos_t, -1.0)
    slot = lax.broadcasted_iota(jnp.int32, (MOE_CHUNK, tm), 0).astype(F32)
    for k in range(max_pieces):
        rows = slice(k * MOE_CHUNK, (k + 1) * MOE_CHUNK)
        e_k = ek_ref[i * max_pieces + k]
        r_k = rk_ref[i * max_pieces + k]

        @pl.when(k < n_pieces)
        def _():
            want = slot + (r_k * MOE_CHUNK).astype(F32)
            p_ref[rows] = jnp.where(pos_ref[pl.ds(e_k, 1), :] == want, 1.0, 0.0).astype(BF16)

        @pl.when(k >= n_pieces)
        def _():
            p_ref[rows] = jnp.zeros((MOE_CHUNK, tm), BF16)

    d = h_ref.shape[1]
    nc = 512
    for c in range(d // nc):
        stage_ref[:, c * nc:(c + 1) * nc] = _dot(p_ref[...], h_ref[:, c * nc:(c + 1) * nc]).astype(BF16)
    c_hi = comb.astype(BF16)
    c_mid = (comb - c_hi.astype(F32)).astype(BF16)
    c_lo = (comb - c_hi.astype(F32) - c_mid.astype(F32)).astype(BF16)
    p = p_ref[...]
    gst_ref[...] = _dot(p, c_hi) + _dot(p, c_mid) + _dot(p, c_lo)

    def copies(k):
        rows = pl.ds(k * MOE_CHUNK, MOE_CHUNK)
        dst = pl.ds(pl.multiple_of(off_ref[i * max_pieces + k], MOE_SEG_ALIGN), MOE_CHUNK)
        return (pltpu.make_async_copy(stage_ref.at[rows], a_out.at[dst], sem.at[0]),
                pltpu.make_async_copy(gst_ref.at[rows], g_out.at[dst], sem.at[1]))

    for k in range(max_pieces):
        @pl.when(k < n_pieces)
        def _():
            for cp in copies(k):
                cp.start()
    for k in range(max_pieces):
        @pl.when(k < n_pieces)
        def _():
            for cp in copies(k):
                cp.wait()


def moe_compact(plan, h, comb, n_rows, tm, max_pieces):
    ek, rk, off, n_pieces = plan
    n, d = h.shape
    a0 = jnp.zeros((n_rows, d), BF16)
    g0 = jnp.zeros((n_rows, LANES), F32)
    any_spec = pl.BlockSpec(memory_space=pl.ANY)
    return pl.pallas_call(
        functools.partial(_moe_compact_kernel, max_pieces=max_pieces),
        grid_spec=pltpu.PrefetchScalarGridSpec(
            num_scalar_prefetch=4,
            grid=(n // tm,),
            in_specs=[pl.BlockSpec((tm, d), lambda i, *_: (i, 0)),
                      pl.BlockSpec((tm, LANES), lambda i, *_: (i, 0)),
                      any_spec, any_spec],
            out_specs=[any_spec, any_spec],
            scratch_shapes=[pltpu.VMEM((LANES, tm), F32),
                            pltpu.VMEM((max_pieces * MOE_CHUNK, tm), BF16),
                            pltpu.VMEM((max_pieces * MOE_CHUNK, d), BF16),
                            pltpu.VMEM((max_pieces * MOE_CHUNK, LANES), F32),
                            pltpu.SemaphoreType.DMA((2,))]),
        out_shape=[jax.ShapeDtypeStruct((n_rows, d), BF16), jax.ShapeDtypeStruct((n_rows, LANES), F32)],
        input_output_aliases={6: 0, 7: 1},
        compiler_params=_params(("arbitrary",), 48),
        name="moe_compact",
    )(ek, rk, off, n_pieces, h, comb, a0, g0)


def _moe_expert_kernel(be_ref, nv_ref, a_ref, gs_ref, wg_ref, wu_ref, wd_ref, y_ref, acc_ref):
    b = pl.program_id(0)
    f = pl.program_id(1)

    @pl.when(b < nv_ref[0])
    def _():
        @pl.when(f == 0)
        def _():
            acc_ref[...] = jnp.zeros(acc_ref.shape, F32)

        a = a_ref[...]
        act = (_silu(_dot(a, wg_ref[0])) * _dot(a, wu_ref[0])).astype(BF16)
        acc_ref[...] += _dot(act, wd_ref[0])

        @pl.when(f == pl.num_programs(1) - 1)
        def _():
            lane = lax.broadcasted_iota(jnp.int32, gs_ref.shape, 1)
            gate = jnp.sum(jnp.where(lane == be_ref[b], gs_ref[...], 0.0), axis=1, keepdims=True)
            y_ref[...] = (acc_ref[...] * gate).astype(y_ref.dtype)

    @pl.when((b >= nv_ref[0]) & (f == 0))
    def _():
        y_ref[...] = jnp.zeros(y_ref.shape, y_ref.dtype)


def moe_experts(blk_exp, n_valid, a_sorted, g_sorted, w_gate, w_up, w_down, tf=512):
    n_rows, d = a_sorted.shape
    n_exp, _, dff = w_gate.shape
    nf = dff // tf
    rb = MOE_ROW_BLOCK

    def blk(b, nv):
        return jnp.minimum(b, nv[0] - 1)

    def fblk(b, f, nv):
        return jnp.where(b < nv[0], f, nf - 1)

    return pl.pallas_call(
        _moe_expert_kernel,
        grid_spec=pltpu.PrefetchScalarGridSpec(
            num_scalar_prefetch=2,
            grid=(n_rows // rb, nf),
            in_specs=[pl.BlockSpec((rb, d), lambda b, f, be, nv: (blk(b, nv), 0)),
                      pl.BlockSpec((rb, LANES), lambda b, f, be, nv: (blk(b, nv), 0)),
                      pl.BlockSpec((1, d, tf), lambda b, f, be, nv: (be[blk(b, nv)], 0, fblk(b, f, nv))),
                      pl.BlockSpec((1, d, tf), lambda b, f, be, nv: (be[blk(b, nv)], 0, fblk(b, f, nv))),
                      pl.BlockSpec((1, tf, d), lambda b, f, be, nv: (be[blk(b, nv)], fblk(b, f, nv), 0))],
            out_specs=pl.BlockSpec((rb, d), lambda b, f, be, nv: (b, 0)),
            scratch_shapes=[pltpu.VMEM((rb, d), F32)]),
        out_shape=jax.ShapeDtypeStruct((n_rows, d), BF16),
        compiler_params=_params(("arbitrary", "arbitrary"), 48),
        name="moe_experts",
    )(blk_exp, n_valid, a_sorted, g_sorted, w_gate, w_up, w_down)


def _moe_combine_kernel(ek_ref, rk_ref, off_ref, np_ref, x_ref, comb_ref, gf_ref, y_hbm, o_ref,
                        posb_ref, sel_ref, ybuf_ref, sem, *, max_pieces, n_exp):
    i = pl.program_id(0)
    tm = x_ref.shape[0]
    n_pieces = np_ref[i]

    def copy(k):
        src = pl.ds(pl.multiple_of(off_ref[i * max_pieces + k], MOE_SEG_ALIGN), MOE_CHUNK)
        return pltpu.make_async_copy(y_hbm.at[src], ybuf_ref.at[pl.ds(k * MOE_CHUNK, MOE_CHUNK)], sem.at[0])

    for k in range(max_pieces):
        @pl.when(k < n_pieces)
        def _():
            copy(k).start()

        @pl.when(k >= n_pieces)
        def _():
            ybuf_ref[k * MOE_CHUNK:(k + 1) * MOE_CHUNK, :] = jnp.zeros((MOE_CHUNK, ybuf_ref.shape[1]), BF16)

    mask = jnp.where(comb_ref[...] > 0.0, 1.0, 0.0)
    r0 = lax.broadcasted_iota(jnp.int32, (tm, tm), 0)
    r1 = lax.broadcasted_iota(jnp.int32, (tm, tm), 1)
    before = jnp.where(r1 < r0, 1.0, 0.0).astype(BF16)
    pos = jnp.where(mask > 0.0, _dot(before, mask.astype(BF16)), -1.0)
    for e in range(n_exp):
        posb_ref[e] = jnp.broadcast_to(pos[:, e:e + 1], (tm, LANES))
    slot = lax.broadcasted_iota(jnp.int32, (tm, MOE_CHUNK), 1).astype(F32)
    for k in range(max_pieces):
        cols = slice(k * MOE_CHUNK, (k + 1) * MOE_CHUNK)
        e_k = ek_ref[i * max_pieces + k]
        r_k = rk_ref[i * max_pieces + k]

        @pl.when(k < n_pieces)
        def _():
            want = slot + (r_k * MOE_CHUNK).astype(F32)
            sel_ref[:, cols] = jnp.where(posb_ref[e_k] == want, 1.0, 0.0).astype(BF16)

        @pl.when(k >= n_pieces)
        def _():
            sel_ref[:, cols] = jnp.zeros((tm, MOE_CHUNK), BF16)

    for k in range(max_pieces):
        @pl.when(k < n_pieces)
        def _():
            copy(k).wait()

    y = x_ref[...] + _dot(sel_ref[...], ybuf_ref[...])
    o_ref[...] = _rms(y, gf_ref[...])


def moe_combine(plan, x, comb, g_final, y_sorted, n_exp, tm, max_pieces):
    ek, rk, off, n_pieces = plan
    n, d = x.shape
    return pl.pallas_call(
        functools.partial(_moe_combine_kernel, max_pieces=max_pieces, n_exp=n_exp),
        grid_spec=pltpu.PrefetchScalarGridSpec(
            num_scalar_prefetch=4,
            grid=(n // tm,),
            in_specs=[pl.BlockSpec((tm, d), lambda i, *_: (i, 0)),
                      pl.BlockSpec((tm, LANES), lambda i, *_: (i, 0)),
                      pl.BlockSpec((1, d), lambda i, *_: (0, 0)),
                      pl.BlockSpec(memory_space=pl.ANY)],
            out_specs=pl.BlockSpec((tm, d), lambda i, *_: (i, 0)),
            scratch_shapes=[pltpu.VMEM((n_exp, tm, LANES), F32),
                            pltpu.VMEM((tm, max_pieces * MOE_CHUNK), BF16),
                            pltpu.VMEM((max_pieces * MOE_CHUNK, d), BF16),
                            pltpu.SemaphoreType.DMA((1,))]),
        out_shape=jax.ShapeDtypeStruct((n, d), F32),
        compiler_params=_params(("arbitrary",), 56),
        name="moe_combine",
    )(ek, rk, off, n_pieces, x, comb, g_final.reshape(1, d), y_sorted)


def moe(x, g, r_hi, r_lo, w_gate, w_up, w_down, g_final, tm=512):
    n, d = x.shape
    n_exp = w_gate.shape[0]
    tm = min(tm, n // 2)
    nt = n // tm
    max_pieces = TOP_K * tm // MOE_CHUNK + n_exp
    n_rows = TOP_K * n + nt * n_exp * (MOE_SEG_ALIGN - 1) + n_exp * (MOE_CHUNK + MOE_ROW_BLOCK)
    n_rows = (n_rows + MOE_ROW_BLOCK - 1) // MOE_ROW_BLOCK * MOE_ROW_BLOCK
    h, comb, cnt = moe_route(x, g, r_hi, r_lo, n_exp, tm)
    cnt = cnt[:, 0, :n_exp].astype(jnp.int32)
    ek, rk, off, n_pieces, blk_exp, n_valid = _moe_plan(cnt, n_rows, max_pieces)
    plan = (ek, rk, off, n_pieces)
    a_sorted, g_sorted = moe_compact(plan, h, comb, n_rows, tm, max_pieces)
    y_sorted = moe_experts(blk_exp, n_valid, a_sorted, g_sorted, w_gate, w_up, w_down)
    return moe_combine(plan, x, comb, g_final, y_sorted, n_exp, tm, max_pieces)


def _pad_cols(w, width):
    return jnp.pad(w, ((0, 0), (0, width - w.shape[1])))


def _even_layer(x2d, bsz, seq, rel_bias, norm_mix, w_in, conv_w, norm_cq, norm_ckv, w_uq, w_uk, w_uv, w_qidx,
                w_out, norm_ffn, ffn_gate, ffn_up, ffn_down):
    nh = w_uq.shape[1]
    a_width = conv_w.shape[1]
    c0 = 3 * a_width
    cq_col, ckv_col = c0, c0 + Q_RANK
    kidx_col = ckv_col + KV_RANK
    widx_col = kidx_col + LANES
    w_in_p = jnp.concatenate([
        w_in[:, :kidx_col],
        _pad_cols(w_in[:, kidx_col:kidx_col + IDX_DIM], LANES),
        _pad_cols(w_in[:, kidx_col + IDX_DIM:], LANES)], axis=1).astype(BF16)
    proj = norm_matmul(x2d, norm_mix, w_in_p).reshape(bsz, seq, -1)

    y_a = conv_mixer(proj, conv_w, a_width)

    w_ukt = jnp.transpose(w_uk, (1, 2, 0)).astype(BF16)
    w_qidx_pad = jnp.pad(w_qidx, ((0, 0), (0, 0), (0, LANES - IDX_DIM))).reshape(Q_RANK, IDX_HEADS * LANES)
    qlat, qidx, ckv, ckvt, wt = dsa_prep(proj, norm_cq, norm_ckv,
                                         w_uq.reshape(Q_RANK, nh * HEAD_DIM).astype(BF16), w_ukt,
                                         w_qidx_pad.astype(BF16), cq_col, ckv_col, widx_col)
    w_uvt = jnp.transpose(w_uv, (1, 2, 0)).astype(BF16)
    y_b = dsa_attention(qlat, qidx, wt, proj, ckv, ckvt, _toeplitz_bias(rel_bias), w_uvt, kidx_col,
                        topk=min(TOPK_MAX, seq // 4))

    n = bsz * seq
    x2d = out_proj(y_a.reshape(n, -1), y_b.reshape(n, -1), w_out.astype(BF16), x2d)
    return ffn(x2d, norm_ffn, ffn_gate.astype(BF16), ffn_up.astype(BF16), ffn_down.astype(BF16))


def _odd_layer(x2d, bsz, seq, norm_mix, w_in, norm_v, w_s, b_s, w_out, norm_ffn, router, exp_gate, exp_up,
               exp_down, final_norm):
    c_width = norm_v.shape[0]
    nh = (w_in.shape[1] - 2 * c_width) // (3 * HEAD_DIM)
    d_width = nh * HEAD_DIM
    proj = norm_matmul(x2d, norm_mix, w_in.astype(BF16))
    y_c = gmlp_mixer(proj, norm_v, w_s, jnp.transpose(b_s), c_width)
    y_d = stick_breaking(proj.reshape(bsz, seq, -1), nh, 2 * c_width, 2 * c_width + d_width,
                         2 * c_width + 2 * d_width)
    x2d = out_proj(y_c, y_d.reshape(bsz * seq, -1), w_out.astype(BF16), x2d)
    r_pad = _pad_cols(router, LANES)
    r_hi = r_pad.astype(BF16)
    r_lo = (r_pad - r_hi.astype(F32)).astype(BF16)
    return moe(x2d, norm_ffn, r_hi, r_lo, exp_gate.astype(BF16), exp_up.astype(BF16), exp_down.astype(BF16),
               final_norm)


def kernel(x, rel_bias, final_norm, e_norm_mix, e_w_in, e_conv_w, e_norm_cq, e_norm_ckv, e_w_uq, e_w_uk, e_w_uv, e_w_qidx, e_w_out, e_norm_ffn, e_ffn_gate, e_ffn_up, e_ffn_down, o_norm_mix, o_w_in, o_norm_v, o_w_s, o_b_s, o_w_out, o_norm_ffn, o_router, o_exp_gate, o_exp_up, o_exp_down):
    bsz, seq, d = x.shape
    assert e_norm_mix.shape[0] == 1 and o_norm_mix.shape[0] == 1, "one even and one odd layer"
    x2d = x.reshape(bsz * seq, d)
    x2d = _even_layer(x2d, bsz, seq, rel_bias, e_norm_mix[0], e_w_in[0], e_conv_w[0], e_norm_cq[0],
                      e_norm_ckv[0], e_w_uq[0], e_w_uk[0], e_w_uv[0], e_w_qidx[0], e_w_out[0], e_norm_ffn[0],
                      e_ffn_gate[0], e_ffn_up[0], e_ffn_down[0])
    out = _odd_layer(x2d, bsz, seq, o_norm_mix[0], o_w_in[0], o_norm_v[0], o_w_s[0], o_b_s[0], o_w_out[0],
                     o_norm_ffn[0], o_router[0], o_exp_gate[0], o_exp_up[0], o_exp_down[0], final_norm)
    return out.reshape(bsz, seq, d)
```

```python
import functools
import math

import jax
import jax.numpy as jnp
from jax import lax
from jax.experimental import pallas as pl
from jax.experimental.pallas import tpu as pltpu

F32 = jnp.float32
BF16 = jnp.bfloat16

EPS = 1e-6
LANES = 128
COUNT_ROWS = 64
HEAD_DIM = 128
Q_RANK = 512
KV_RANK = 256
IDX_HEADS = 16
IDX_DIM = 64
IDX_SCALE = (IDX_DIM ** -0.5) * (IDX_HEADS ** -0.5)
TOPK_MAX = 256
DSA_KEY_CHUNK = 256
REL_BUCKETS = 32
REL_MAX_DIST = 128
CONV_WIDTH = 3
CHUNK = 128
TOP_K = 2
INT_MIN = -(2 ** 31)
MIB = 1024 * 1024


def _params(sem, vmem_mib):
    return pltpu.CompilerParams(dimension_semantics=sem, vmem_limit_bytes=vmem_mib * MIB)


def _rms(x, g):
    return x * lax.rsqrt(jnp.mean(x * x, axis=-1, keepdims=True) + EPS) * g


def _dot(a, b):
    return jnp.dot(a, b, preferred_element_type=F32)


def _dot_nt(a, b):
    return lax.dot_general(a, b, (((1,), (1,)), ((), ())), preferred_element_type=F32)


def _norm_matmul_kernel(x_ref, g_ref, w_ref, o_ref, h_ref):
    @pl.when(pl.program_id(1) == 0)
    def _():
        h_ref[...] = _rms(x_ref[...], g_ref[...]).astype(BF16)

    o_ref[...] = _dot(h_ref[...], w_ref[...]).astype(o_ref.dtype)


def norm_matmul(x, g, w, tm=1024, tn=1024):
    n, d = x.shape
    nout = w.shape[1]
    return pl.pallas_call(
        _norm_matmul_kernel,
        grid=(n // tm, nout // tn),
        in_specs=[pl.BlockSpec((tm, d), lambda i, j: (i, 0)),
                  pl.BlockSpec((1, d), lambda i, j: (0, 0)),
                  pl.BlockSpec((d, tn), lambda i, j: (0, j))],
        out_specs=pl.BlockSpec((tm, tn), lambda i, j: (i, j)),
        out_shape=jax.ShapeDtypeStruct((n, nout), BF16),
        scratch_shapes=[pltpu.VMEM((tm, d), BF16)],
        compiler_params=_params(("parallel", "arbitrary"), 40),
        name="norm_matmul",
    )(x, g.reshape(1, d), w)


def _conv_kernel(b_ref, c_ref, x_ref, w_ref, o_ref):
    z = c_ref[0].astype(F32) * x_ref[0].astype(F32)
    row = lax.broadcasted_iota(jnp.int32, z.shape, 0)
    y = w_ref[CONV_WIDTH - 1:CONV_WIDTH, :] * z
    for lag in range(1, CONV_WIDTH):
        zl = jnp.where(row >= lag, pltpu.roll(z, lag, 0), 0.0)
        y = y + w_ref[CONV_WIDTH - 1 - lag:CONV_WIDTH - lag, :] * zl
    o_ref[0] = (b_ref[0].astype(F32) * y).astype(o_ref.dtype)


def conv_mixer(proj, conv_w, a_width, tc=256):
    bsz, seq, _ = proj.shape
    nb = a_width // tc
    return pl.pallas_call(
        _conv_kernel,
        grid=(bsz, nb),
        in_specs=[pl.BlockSpec((1, seq, tc), lambda b, c: (b, 0, c)),
                  pl.BlockSpec((1, seq, tc), lambda b, c: (b, 0, nb + c)),
                  pl.BlockSpec((1, seq, tc), lambda b, c: (b, 0, 2 * nb + c)),
                  pl.BlockSpec((CONV_WIDTH, tc), lambda b, c: (0, c))],
        out_specs=pl.BlockSpec((1, seq, tc), lambda b, c: (b, 0, c)),
        out_shape=jax.ShapeDtypeStruct((bsz, seq, a_width), BF16),
        compiler_params=_params(("parallel", "parallel"), 40),
        name="conv_mixer",
    )(proj, proj, proj, conv_w)


def _dsa_prep_kernel(cq_ref, ckv_ref, w_ref, gq_ref, gkv_ref, wuq_ref, wukt_ref, wqi_ref,
                     qlat_ref, qidx_ref, ckvo_ref, ckvt_ref, wt_ref):
    tm = cq_ref.shape[1]
    cqn = _rms(cq_ref[0].astype(F32), gq_ref[...]).astype(BF16)
    ckvn = _rms(ckv_ref[0].astype(F32), gkv_ref[...])
    ckvo_ref[0] = ckvn.astype(BF16)
    for c in range(tm // DSA_KEY_CHUNK):
        rows = slice(c * DSA_KEY_CHUNK, (c + 1) * DSA_KEY_CHUNK)
        ckvt_ref[0, c] = jnp.transpose(ckvn[rows, :]).astype(BF16)
    wt_ref[0] = jnp.transpose(w_ref[0].astype(F32))[:IDX_HEADS, :] * IDX_SCALE
    q = _dot(cqn, wuq_ref[...]).astype(BF16)
    scale = HEAD_DIM ** -0.5
    for h in range(wukt_ref.shape[0]):
        qh = q[:, h * HEAD_DIM:(h + 1) * HEAD_DIM]
        qlat_ref[0, h] = (_dot(qh, wukt_ref[h]) * scale).astype(BF16)
    qi = _dot(cqn, wqi_ref[...]).astype(BF16)
    for g in range(IDX_HEADS):
        qidx_ref[0, g] = qi[:, g * LANES:(g + 1) * LANES]


def dsa_prep(proj, g_cq, g_ckv, w_uq, w_ukt, w_qidx_pad, cq_col, ckv_col, widx_col, tm=256):
    bsz, seq, _ = proj.shape
    nh = w_ukt.shape[0]
    full = lambda *shape: pl.BlockSpec(shape, lambda b, i: (0,) * len(shape))
    return pl.pallas_call(
        _dsa_prep_kernel,
        grid=(bsz, seq // tm),
        in_specs=[pl.BlockSpec((1, tm, Q_RANK), lambda b, i: (b, i, cq_col // Q_RANK)),
                  pl.BlockSpec((1, tm, KV_RANK), lambda b, i: (b, i, ckv_col // KV_RANK)),
                  pl.BlockSpec((1, tm, LANES), lambda b, i: (b, i, widx_col // LANES)),
                  full(1, Q_RANK), full(1, KV_RANK),
                  full(Q_RANK, nh * HEAD_DIM), full(nh, HEAD_DIM, KV_RANK),
                  full(Q_RANK, IDX_HEADS * LANES)],
        out_specs=[pl.BlockSpec((1, nh, tm, KV_RANK), lambda b, i: (b, 0, i, 0)),
                   pl.BlockSpec((1, IDX_HEADS, tm, LANES), lambda b, i: (b, 0, i, 0)),
                   pl.BlockSpec((1, tm, KV_RANK), lambda b, i: (b, i, 0)),
                   pl.BlockSpec((1, tm // DSA_KEY_CHUNK, KV_RANK, DSA_KEY_CHUNK), lambda b, i: (b, i, 0, 0)),
                   pl.BlockSpec((1, IDX_HEADS, tm), lambda b, i: (b, 0, i))],
        out_shape=[jax.ShapeDtypeStruct((bsz, nh, seq, KV_RANK), BF16),
                   jax.ShapeDtypeStruct((bsz, IDX_HEADS, seq, LANES), BF16),
                   jax.ShapeDtypeStruct((bsz, seq, KV_RANK), BF16),
                   jax.ShapeDtypeStruct((bsz, seq // DSA_KEY_CHUNK, KV_RANK, DSA_KEY_CHUNK), BF16),
                   jax.ShapeDtypeStruct((bsz, IDX_HEADS, seq), F32)],
        compiler_params=_params(("parallel", "parallel"), 40),
        name="dsa_prep",
    )(proj, proj, proj, g_cq.reshape(1, -1), g_ckv.reshape(1, -1), w_uq, w_ukt, w_qidx_pad)


def _dsa_kernel(qlat_ref, qidx_ref, wt_ref, kidx_ref, ckv_ref, ckvt_ref, toep_ref, wuvt_ref, o_ref,
                sc_ref, key_ref, p_ref, m_ref, l_ref, alpha_ref, acc_ref, *, topk):
    tq = LANES
    tk = DSA_KEY_CHUNK
    nh = qlat_ref.shape[1]
    i = pl.program_id(1)
    n_chunks = (i * tq) // tk + 1
    kpos = lax.broadcasted_iota(jnp.int32, (tk, tq), 0)
    qpos = lax.broadcasted_iota(jnp.int32, (tk, tq), 1)
    wt = wt_ref[0]
    qi = qidx_ref[0].reshape(IDX_HEADS * tq, LANES)

    def score_body(c, carry):
        kj = kidx_ref[0, pl.ds(pl.multiple_of(c * tk, tk), tk), :]
        sg = _dot_nt(kj, qi)
        s = jnp.zeros((tk, tq), F32)
        for g in range(IDX_HEADS):
            s = s + wt[g:g + 1, :] * jnp.maximum(sg[:, g * tq:(g + 1) * tq], 0.0)
        sc_ref[c] = s
        bits = pltpu.bitcast(s, jnp.int32)
        key = bits ^ ((bits >> 31) & 0x7FFFFFFF)
        causal = (c * tk + kpos) <= (i * tq + qpos)
        key_ref[c] = jnp.where(causal, key, INT_MIN)
        return carry

    lax.fori_loop(0, n_chunks, score_body, 0)

    kvec = jnp.minimum(topk, i * tq + qpos[:1, :] + 1).astype(F32)

    def count_ge(cand):
        def body(c, acc):
            hit = jnp.where(key_ref[c] >= cand, 1.0, 0.0)
            return acc + jnp.sum(hit.reshape(tk // COUNT_ROWS, COUNT_ROWS, tq), axis=0)
        acc = lax.fori_loop(0, n_chunks, body, jnp.zeros((COUNT_ROWS, tq), F32))
        return jnp.sum(acc, axis=0, keepdims=True)

    thr = jnp.where(count_ge(jnp.zeros((1, tq), jnp.int32)) >= kvec, 0, INT_MIN).astype(jnp.int32)

    def bit_body(b, thr):
        cand = thr | jnp.left_shift(jnp.int32(1), 30 - b)
        return jnp.where(count_ge(cand) >= kvec, cand, thr)

    thr = lax.fori_loop(0, 31, bit_body, thr)

    m_ref[...] = jnp.full(m_ref.shape, -jnp.inf, F32)
    l_ref[...] = jnp.zeros(l_ref.shape, F32)
    acc_ref[...] = jnp.zeros(acc_ref.shape, F32)
    q = qlat_ref[0].reshape(nh * tq, KV_RANK)

    def att_body(c, carry):
        cj = ckv_ref[0, pl.ds(pl.multiple_of(c * tk, tk), tk), :]
        lg = _dot_nt(cj, q)
        sel = key_ref[c] >= thr
        sj = sc_ref[c]
        mdist = [jnp.clip(i - (c * (tk // tq) + r), 0, 2) for r in range(tk // tq)]
        for h in range(nh):
            cols = slice(h * tq, (h + 1) * tq)
            bias = jnp.concatenate([toep_ref[h, m] for m in mdist], axis=0)
            lh = jnp.where(sel, lg[:, cols] + bias + sj, -jnp.inf)
            m_old = m_ref[:, cols]
            m_new = jnp.maximum(m_old, jnp.max(lh, axis=0, keepdims=True))
            m_safe = jnp.where(m_new == -jnp.inf, 0.0, m_new)
            alpha = jnp.exp(m_old - m_safe)
            p = jnp.exp(lh - m_safe)
            m_ref[:, cols] = m_new
            l_ref[:, cols] = alpha * l_ref[:, cols] + jnp.sum(p, axis=0, keepdims=True)
            alpha_ref[:, cols] = alpha
            p_ref[:, cols] = p.astype(BF16)
        acc_ref[...] = alpha_ref[...] * acc_ref[...] + _dot(ckvt_ref[0, c], p_ref[...])
        return carry

    lax.fori_loop(0, n_chunks, att_body, 0)

    for h in range(nh):
        cols = slice(h * tq, (h + 1) * tq)
        o_lat_t = (acc_ref[:, cols] / l_ref[:, cols]).astype(BF16)
        y_t = _dot(wuvt_ref[h], o_lat_t)
        o_ref[0, :, h * HEAD_DIM:(h + 1) * HEAD_DIM] = jnp.transpose(y_t).astype(o_ref.dtype)


def dsa_attention(qlat, qidx, wt, proj, ckv, ckvt, toep, w_uvt, kidx_col, topk):
    bsz, nh, seq, _ = qlat.shape
    tq = LANES
    tk = DSA_KEY_CHUNK
    nkb = seq // tk
    full = lambda *shape: pl.BlockSpec(shape, lambda b, i: (0,) * len(shape))
    return pl.pallas_call(
        functools.partial(_dsa_kernel, topk=topk),
        grid=(bsz, seq // tq),
        in_specs=[pl.BlockSpec((1, nh, tq, KV_RANK), lambda b, i: (b, 0, i, 0)),
                  pl.BlockSpec((1, IDX_HEADS, tq, LANES), lambda b, i: (b, 0, i, 0)),
                  pl.BlockSpec((1, IDX_HEADS, tq), lambda b, i: (b, 0, i)),
                  pl.BlockSpec((1, seq, LANES), lambda b, i: (b, 0, kidx_col // LANES)),
                  pl.BlockSpec((1, seq, KV_RANK), lambda b, i: (b, 0, 0)),
                  pl.BlockSpec((1, nkb, KV_RANK, tk), lambda b, i: (b, 0, 0, 0)),
                  full(nh, 3, tq, tq),
                  full(nh, HEAD_DIM, KV_RANK)],
        out_specs=pl.BlockSpec((1, tq, nh * HEAD_DIM), lambda b, i: (b, i, 0)),
        out_shape=jax.ShapeDtypeStruct((bsz, seq, nh * HEAD_DIM), BF16),
        scratch_shapes=[pltpu.VMEM((nkb, tk, tq), F32),
                        pltpu.VMEM((nkb, tk, tq), jnp.int32),
                        pltpu.VMEM((tk, nh * tq), BF16),
                        pltpu.VMEM((1, nh * tq), F32),
                        pltpu.VMEM((1, nh * tq), F32),
                        pltpu.VMEM((1, nh * tq), F32),
                        pltpu.VMEM((KV_RANK, nh * tq), F32)],
        compiler_params=_params(("parallel", "arbitrary"), 40),
        name="dsa_attention",
    )(qlat, qidx, wt, proj, ckv, ckvt, toep, w_uvt)


def _rel_bucket(dist):
    n = jnp.maximum(dist, 0)
    max_exact = REL_BUCKETS // 2
    nf = jnp.maximum(n, max_exact).astype(F32)
    large = max_exact + (jnp.log(nf / max_exact) / math.log(REL_MAX_DIST / max_exact)
                         * (REL_BUCKETS - max_exact)).astype(jnp.int32)
    large = jnp.minimum(large, REL_BUCKETS - 1)
    return jnp.where(n < max_exact, n, large)


def _toeplitz_bias(rel_bias):
    t = LANES
    r = jnp.arange(t, dtype=jnp.int32)
    tiles = []
    for m in range(3):
        dist = m * t + r[None, :] - r[:, None] if m < 2 else jnp.full((t, t), 2 * REL_MAX_DIST, jnp.int32)
        tiles.append(rel_bias[_rel_bucket(dist)])
    return jnp.transpose(jnp.stack(tiles), (3, 0, 1, 2)).astype(F32)


def _out_proj_kernel(ya_ref, yb_ref, wa_ref, wb_ref, x_ref, o_ref):
    o_ref[...] = x_ref[...] + _dot(ya_ref[...], wa_ref[...]) + _dot(yb_ref[...], wb_ref[...])


def out_proj(ya, yb, w_out, x, tm=512, tn=1024):
    n, ka = ya.shape
    kb = yb.shape[1]
    d = x.shape[1]
    wa, wb = w_out[:ka], w_out[ka:]
    return pl.pallas_call(
        _out_proj_kernel,
        grid=(n // tm, d // tn),
        in_specs=[pl.BlockSpec((tm, ka), lambda i, j: (i, 0)),
                  pl.BlockSpec((tm, kb), lambda i, j: (i, 0)),
                  pl.BlockSpec((ka, tn), lambda i, j: (0, j)),
                  pl.BlockSpec((kb, tn), lambda i, j: (0, j)),
                  pl.BlockSpec((tm, tn), lambda i, j: (i, j))],
        out_specs=pl.BlockSpec((tm, tn), lambda i, j: (i, j)),
        out_shape=jax.ShapeDtypeStruct((n, d), F32),
        compiler_params=_params(("parallel", "parallel"), 40),
        name="out_proj",
    )(ya, yb, wa, wb, x)


def _silu(a):
    return a / (1.0 + jnp.exp(-a))


def _ffn_kernel(x_ref, g_ref, wg_ref, wu_ref, wd_ref, o_ref, h_ref):
    @pl.when(pl.program_id(1) == 0)
    def _():
        x = x_ref[...]
        h_ref[...] = _rms(x, g_ref[...]).astype(BF16)
        o_ref[...] = x

    h = h_ref[...]
    act = (_silu(_dot(h, wg_ref[...])) * _dot(h, wu_ref[...])).astype(BF16)
    o_ref[...] += _dot(act, wd_ref[...])


def ffn(x, g, w_gate, w_up, w_down, tm=1024, tf=512):
    n, d = x.shape
    dff = w_gate.shape[1]
    return pl.pallas_call(
        _ffn_kernel,
        grid=(n // tm, dff // tf),
        in_specs=[pl.BlockSpec((tm, d), lambda i, f: (i, 0)),
                  pl.BlockSpec((1, d), lambda i, f: (0, 0)),
                  pl.BlockSpec((d, tf), lambda i, f: (0, f)),
                  pl.BlockSpec((d, tf), lambda i, f: (0, f)),
                  pl.BlockSpec((tf, d), lambda i, f: (f, 0))],
        out_specs=pl.BlockSpec((tm, d), lambda i, f: (i, 0)),
        out_shape=jax.ShapeDtypeStruct((n, d), F32),
        scratch_shapes=[pltpu.VMEM((tm, d), BF16)],
        compiler_params=_params(("parallel", "arbitrary"), 58),
        name="ffn",
    )(x, g.reshape(1, d), w_gate, w_up, w_down)


def _gelu(x):
    return x * (0.5 * (1.0 + jnp.tanh(math.sqrt(2.0 / math.pi) * (x + 0.044715 * (x * x * x)))))


def _gmlp_kernel(cu_ref, cv_ref, gv_ref, ws_ref, bst_ref, o_ref):
    tm = cu_ref.shape[0]
    ng = ws_ref.shape[0]
    gw = cu_ref.shape[1] // ng
    u = _gelu(cu_ref[...].astype(F32))
    vn = _rms(_gelu(cv_ref[...].astype(F32)), gv_ref[...]).astype(BF16)
    row = lax.broadcasted_iota(jnp.int32, (CHUNK, CHUNK), 0)
    col = lax.broadcasted_iota(jnp.int32, (CHUNK, CHUNK), 1)
    for g in range(ng):
        wsg = jnp.where(row >= col, ws_ref[g], 0.0).astype(BF16)
        bias = bst_ref[:, g:g + 1]
        cols = slice(g * gw, (g + 1) * gw)
        for c in range(tm // CHUNK):
            rows = slice(c * CHUNK, (c + 1) * CHUNK)
            mixed = _dot(wsg, vn[rows, cols]) + bias
            o_ref[rows, cols] = (u[rows, cols] * mixed).astype(o_ref.dtype)


def gmlp_mixer(proj, g_v, w_s, b_s_t, c_width, tm=256):
    n = proj.shape[0]
    ng = w_s.shape[0]
    return pl.pallas_call(
        _gmlp_kernel,
        grid=(n // tm,),
        in_specs=[pl.BlockSpec((tm, c_width), lambda i: (i, 0)),
                  pl.BlockSpec((tm, c_width), lambda i: (i, 1)),
                  pl.BlockSpec((1, c_width), lambda i: (0, 0)),
                  pl.BlockSpec((ng, CHUNK, CHUNK), lambda i: (0, 0, 0)),
                  pl.BlockSpec((CHUNK, ng), lambda i: (0, 0))],
        out_specs=pl.BlockSpec((tm, c_width), lambda i: (i, 0)),
        out_shape=jax.ShapeDtypeStruct((n, c_width), BF16),
        compiler_params=_params(("parallel",), 40),
        name="gmlp_mixer",
    )(proj, proj, g_v.reshape(1, -1), w_s, b_s_t)


def _sb_kernel(q_ref, k_ref, v_ref, u_ref, o_ref, acc_ref, carry_ref, *, tk):
    tq = q_ref.shape[1]
    i = pl.program_id(2)
    acc_ref[...] = jnp.zeros(acc_ref.shape, F32)
    carry_ref[...] = jnp.zeros(carry_ref.shape, F32)
    q = q_ref[0]
    tpos = i * tq + lax.broadcasted_iota(jnp.int32, (tq, tk), 0)
    col = lax.broadcasted_iota(jnp.int32, (tq, tk), 1)
    n_diag = tq // tk

    def block(j, diagonal):
        start = pl.multiple_of(j * tk, tk)
        kj = k_ref[0, pl.ds(start, tk), :]
        vj = v_ref[0, pl.ds(start, tk), :]
        z = _dot_nt(q, kj)
        log_keep = -(jnp.maximum(z, 0.0) + jnp.log(1.0 + jnp.exp(-jnp.abs(z))))
        log_beta = z + log_keep
        if diagonal:
            past = (j * tk + col) < tpos
            log_keep = jnp.where(past, log_keep, 0.0)
        hi = log_keep.astype(BF16)
        lo = (log_keep - hi.astype(F32)).astype(BF16)
        excl = _dot(hi, u_ref[...]) + _dot(lo, u_ref[...])
        wts = jnp.exp(log_beta + excl + carry_ref[...])
        if diagonal:
            wts = jnp.where(past, wts, 0.0)
        acc_ref[...] += _dot(wts.astype(BF16), vj)
        carry_ref[...] += jnp.sum(log_keep, axis=1, keepdims=True)

    for d in range(n_diag):
        block((i + 1) * n_diag - 1 - d, True)

    def body(step, c):
        block(i * n_diag - 1 - step, False)
        return c

    lax.fori_loop(0, i * n_diag, body, 0)
    o_ref[0] = acc_ref[...].astype(o_ref.dtype)


def stick_breaking(proj, nh, q_col, k_col, v_col, tq=512, tk=256):
    bsz, seq, _ = proj.shape
    tq = min(tq, seq)
    tk = min(tk, tq)
    r = jnp.arange(tk, dtype=jnp.int32)
    upper = (r[:, None] > r[None, :]).astype(BF16)
    qb, kb, vb = q_col // HEAD_DIM, k_col // HEAD_DIM, v_col // HEAD_DIM
    return pl.pallas_call(
        functools.partial(_sb_kernel, tk=tk),
        grid=(bsz, nh, seq // tq),
        in_specs=[pl.BlockSpec((1, tq, HEAD_DIM), lambda b, h, i: (b, i, qb + h)),
                  pl.BlockSpec((1, seq, HEAD_DIM), lambda b, h, i: (b, 0, kb + h)),
                  pl.BlockSpec((1, seq, HEAD_DIM), lambda b, h, i: (b, 0, vb + h)),
                  pl.BlockSpec((tk, tk), lambda b, h, i: (0, 0))],
        out_specs=pl.BlockSpec((1, tq, HEAD_DIM), lambda b, h, i: (b, i, h)),
        out_shape=jax.ShapeDtypeStruct((bsz, seq, nh * HEAD_DIM), BF16),
        scratch_shapes=[pltpu.VMEM((tq, HEAD_DIM), F32), pltpu.VMEM((tq, 1), F32)],
        compiler_params=_params(("parallel", "parallel", "arbitrary"), 40),
        name="stick_breaking",
    )(proj, proj, proj, upper)


MOE_CHUNK = 128
MOE_SEG_ALIGN = 16
MOE_ROW_BLOCK = 1024
MOE_SUB_BLOCK = 256


def _moe_route_kernel(x_ref, g_ref, rhi_ref, rlo_ref, h_ref, comb_ref, cnt_ref, *, n_exp):
    tm = x_ref.shape[0]
    h32 = _rms(x_ref[...], g_ref[...])
    hi = h32.astype(BF16)
    lo = (h32 - hi.astype(F32)).astype(BF16)
    h_ref[...] = hi
    logits = _dot(hi, rhi_ref[...]) + (_dot(hi, rlo_ref[...]) + _dot(lo, rhi_ref[...]))
    lane = lax.broadcasted_iota(jnp.int32, (tm, LANES), 1)
    lanef = lane.astype(F32)
    lg = jnp.where(lane < n_exp, logits, -jnp.inf)
    m1 = jnp.max(lg, axis=1, keepdims=True)
    i1 = jnp.min(jnp.where(lg == m1, lanef, float(LANES)), axis=1, keepdims=True)
    lg2 = jnp.where(lanef == i1, -jnp.inf, lg)
    m2 = jnp.max(lg2, axis=1, keepdims=True)
    i2 = jnp.min(jnp.where(lg2 == m2, lanef, float(LANES)), axis=1, keepdims=True)
    e2 = jnp.exp(m2 - m1)
    g1 = 1.0 / (1.0 + e2)
    comb = jnp.where(lanef == i1, g1, 0.0) + jnp.where(lanef == i2, e2 * g1, 0.0)
    comb_ref[...] = comb
    cnt_ref[0] = jnp.sum(jnp.where(comb > 0.0, 1.0, 0.0), axis=0, keepdims=True)


def moe_route(x, g, r_hi, r_lo, n_exp, tm):
    n, d = x.shape
    nt = n // tm
    return pl.pallas_call(
        functools.partial(_moe_route_kernel, n_exp=n_exp),
        grid=(nt,),
        in_specs=[pl.BlockSpec((tm, d), lambda i: (i, 0)),
                  pl.BlockSpec((1, d), lambda i: (0, 0)),
                  pl.BlockSpec((d, LANES), lambda i: (0, 0)),
                  pl.BlockSpec((d, LANES), lambda i: (0, 0))],
        out_specs=[pl.BlockSpec((tm, d), lambda i: (i, 0)),
                   pl.BlockSpec((tm, LANES), lambda i: (i, 0)),
                   pl.BlockSpec((1, 1, LANES), lambda i: (i, 0, 0))],
        out_shape=[jax.ShapeDtypeStruct((n, d), BF16),
                   jax.ShapeDtypeStruct((n, LANES), F32),
                   jax.ShapeDtypeStruct((nt, 1, LANES), F32)],
        compiler_params=_params(("parallel",), 40),
        name="moe_route",
    )(x, g.reshape(1, d), r_hi, r_lo)


def _moe_plan(cnt, n_rows_static, max_pieces):
    nt, n_exp = cnt.shape
    i32 = jnp.int32
    cp = (cnt + MOE_SEG_ALIGN - 1) // MOE_SEG_ALIGN * MOE_SEG_ALIGN
    reg = (jnp.sum(cp, axis=0) + MOE_CHUNK + MOE_ROW_BLOCK - 1) // MOE_ROW_BLOCK * MOE_ROW_BLOCK
    base = jnp.cumsum(reg) - reg
    seg = base[None, :] + jnp.cumsum(cp, axis=0) - cp
    nce = (cnt + MOE_CHUNK - 1) // MOE_CHUNK
    cum = jnp.cumsum(nce, axis=1)
    n_pieces = cum[:, -1]
    k = jnp.arange(max_pieces, dtype=i32)
    ek = jnp.minimum(jnp.sum(cum[:, None, :] <= k[None, :, None], axis=-1), n_exp - 1).astype(i32)
    is_e = ek[:, :, None] == jnp.arange(n_exp, dtype=i32)
    pick = lambda tbl: jnp.sum(jnp.where(is_e, tbl[:, None, :], 0), axis=-1)
    rk = k[None, :] - pick(cum - nce)
    off = pick(seg) + rk * MOE_CHUNK
    valid = k[None, :] < n_pieces[:, None]
    rk = jnp.where(valid, rk, 0).astype(i32)
    off = jnp.where(valid, off, 0).astype(i32)
    n_blocks = n_rows_static // MOE_ROW_BLOCK
    cb = jnp.cumsum(reg // MOE_ROW_BLOCK)
    blk = jnp.arange(n_blocks, dtype=i32)
    blk_exp = jnp.minimum(jnp.sum(cb[None, :] <= blk[:, None], axis=1), n_exp - 1)
    is_be = blk_exp[:, None] == jnp.arange(n_exp, dtype=i32)
    pick_b = lambda vec: jnp.sum(jnp.where(is_be, vec[None, :], 0), axis=-1)
    used = jnp.sum(cp, axis=0)
    blk_rows = jnp.clip(pick_b(used) - (blk * MOE_ROW_BLOCK - pick_b(base)), 0, MOE_ROW_BLOCK)
    return (ek.reshape(-1), rk.reshape(-1), off.reshape(-1), n_pieces.astype(i32),
            blk_exp.astype(i32), blk_rows.astype(i32), cb[-1:].astype(i32))


def _moe_compact_kernel(ek_ref, rk_ref, off_ref, np_ref, h_ref, comb_ref, a_in, g_in, a_out, g_out,
                        pos_ref, p_ref, stage_ref, gst_ref, sem, *, max_pieces):
    del a_in, g_in
    i = pl.program_id(0)
    tm = h_ref.shape[0]
    n_pieces = np_ref[i]
    comb = comb_ref[...]
    mask_t = jnp.transpose(jnp.where(comb > 0.0, 1.0, 0.0))
    r0 = lax.broadcasted_iota(jnp.int32, (tm, tm), 0)
    r1 = lax.broadcasted_iota(jnp.int32, (tm, tm), 1)
    before = jnp.where(r0 < r1, 1.0, 0.0).astype(BF16)
    pos_t = _dot(mask_t.astype(BF16), before)
    pos_ref[...] = jnp.where(mask_t > 0.0, pos_t, -1.0)
    slot = lax.broadcasted_iota(jnp.int32, (MOE_CHUNK, tm), 0).astype(F32)
    for k in range(max_pieces):
        rows = slice(k * MOE_CHUNK, (k + 1) * MOE_CHUNK)
        e_k = ek_ref[i * max_pieces + k]
        r_k = rk_ref[i * max_pieces + k]

        @pl.when(k < n_pieces)
        def _():
            want = slot + (r_k * MOE_CHUNK).astype(F32)
            p_ref[rows] = jnp.where(pos_ref[pl.ds(e_k, 1), :] == want, 1.0, 0.0).astype(BF16)

        @pl.when(k >= n_pieces)
        def _():
            p_ref[rows] = jnp.zeros((MOE_CHUNK, tm), BF16)

    d = h_ref.shape[1]
    nc = 512
    for c in range(d // nc):
        stage_ref[:, c * nc:(c + 1) * nc] = _dot(p_ref[...], h_ref[:, c * nc:(c + 1) * nc]).astype(BF16)
    c_hi = comb.astype(BF16)
    c_mid = (comb - c_hi.astype(F32)).astype(BF16)
    c_lo = (comb - c_hi.astype(F32) - c_mid.astype(F32)).astype(BF16)
    p = p_ref[...]
    gst_ref[...] = _dot(p, c_hi) + _dot(p, c_mid) + _dot(p, c_lo)

    def copies(k):
        rows = pl.ds(k * MOE_CHUNK, MOE_CHUNK)
        dst = pl.ds(pl.multiple_of(off_ref[i * max_pieces + k], MOE_SEG_ALIGN), MOE_CHUNK)
        return (pltpu.make_async_copy(stage_ref.at[rows], a_out.at[dst], sem.at[0]),
                pltpu.make_async_copy(gst_ref.at[rows], g_out.at[dst], sem.at[1]))

    for k in range(max_pieces):
        @pl.when(k < n_pieces)
        def _():
            for cp in copies(k):
                cp.start()
    for k in range(max_pieces):
        @pl.when(k < n_pieces)
        def _():
            for cp in copies(k):
                cp.wait()


def moe_compact(plan, h, comb, n_rows, tm, max_pieces):
    ek, rk, off, n_pieces = plan
    n, d = h.shape
    a0 = jnp.zeros((n_rows, d), BF16)
    g0 = jnp.zeros((n_rows, LANES), F32)
    any_spec = pl.BlockSpec(memory_space=pl.ANY)
    return pl.pallas_call(
        functools.partial(_moe_compact_kernel, max_pieces=max_pieces),
        grid_spec=pltpu.PrefetchScalarGridSpec(
            num_scalar_prefetch=4,
            grid=(n // tm,),
            in_specs=[pl.BlockSpec((tm, d), lambda i, *_: (i, 0)),
                      pl.BlockSpec((tm, LANES), lambda i, *_: (i, 0)),
                      any_spec, any_spec],
            out_specs=[any_spec, any_spec],
            scratch_shapes=[pltpu.VMEM((LANES, tm), F32),
                            pltpu.VMEM((max_pieces * MOE_CHUNK, tm), BF16),
                            pltpu.VMEM((max_pieces * MOE_CHUNK, d), BF16),
                            pltpu.VMEM((max_pieces * MOE_CHUNK, LANES), F32),
                            pltpu.SemaphoreType.DMA((2,))]),
        out_shape=[jax.ShapeDtypeStruct((n_rows, d), BF16), jax.ShapeDtypeStruct((n_rows, LANES), F32)],
        input_output_aliases={6: 0, 7: 1},
        compiler_params=_params(("arbitrary",), 48),
        name="moe_compact",
    )(ek, rk, off, n_pieces, h, comb, a0, g0)


def _moe_expert_kernel(be_ref, br_ref, nv_ref, a_ref, gs_ref, wg_ref, wu_ref, wd_ref, y_ref, acc_ref):
    b = pl.program_id(0)
    f = pl.program_id(1)
    rb = a_ref.shape[0]
    n_rows = br_ref[b]
    expert = be_ref[b]
    busy = b < nv_ref[0]

    def step(rows):
        @pl.when(f == 0)
        def _():
            acc_ref[rows] = jnp.zeros((rows.stop - rows.start, acc_ref.shape[1]), F32)

        a = a_ref[rows]
        act = (_silu(_dot(a, wg_ref[0])) * _dot(a, wu_ref[0])).astype(BF16)
        acc_ref[rows] += _dot(act, wd_ref[0])

        @pl.when(f == pl.num_programs(1) - 1)
        def _():
            gs = gs_ref[rows]
            lane = lax.broadcasted_iota(jnp.int32, gs.shape, 1)
            gate = jnp.sum(jnp.where(lane == expert, gs, 0.0), axis=1, keepdims=True)
            y_ref[rows] = (acc_ref[rows] * gate).astype(y_ref.dtype)

    @pl.when(busy & (n_rows == rb))
    def _():
        step(slice(0, rb))

    for sb in range(rb // MOE_SUB_BLOCK):
        rows = slice(sb * MOE_SUB_BLOCK, (sb + 1) * MOE_SUB_BLOCK)

        @pl.when(busy & (n_rows < rb) & (sb * MOE_SUB_BLOCK < n_rows))
        def _():
            step(rows)

        @pl.when((jnp.logical_not(busy) | (sb * MOE_SUB_BLOCK >= n_rows)) & (f == 0))
        def _():
            y_ref[rows] = jnp.zeros((MOE_SUB_BLOCK, y_ref.shape[1]), y_ref.dtype)


def moe_experts(blk_exp, blk_rows, n_valid, a_sorted, g_sorted, w_gate, w_up, w_down, tf=512):
    n_rows, d = a_sorted.shape
    n_exp, _, dff = w_gate.shape
    nf = dff // tf
    rb = MOE_ROW_BLOCK

    def blk(b, nv):
        return jnp.minimum(b, nv[0] - 1)

    def fblk(b, f, nv):
        return jnp.where(b < nv[0], f, nf - 1)

    return pl.pallas_call(
        _moe_expert_kernel,
        grid_spec=pltpu.PrefetchScalarGridSpec(
            num_scalar_prefetch=3,
            grid=(n_rows // rb, nf),
            in_specs=[pl.BlockSpec((rb, d), lambda b, f, be, br, nv: (blk(b, nv), 0)),
                      pl.BlockSpec((rb, LANES), lambda b, f, be, br, nv: (blk(b, nv), 0)),
                      pl.BlockSpec((1, d, tf), lambda b, f, be, br, nv: (be[blk(b, nv)], 0, fblk(b, f, nv))),
                      pl.BlockSpec((1, d, tf), lambda b, f, be, br, nv: (be[blk(b, nv)], 0, fblk(b, f, nv))),
                      pl.BlockSpec((1, tf, d), lambda b, f, be, br, nv: (be[blk(b, nv)], fblk(b, f, nv), 0))],
            out_specs=pl.BlockSpec((rb, d), lambda b, f, be, br, nv: (b, 0)),
            scratch_shapes=[pltpu.VMEM((rb, d), F32)]),
        out_shape=jax.ShapeDtypeStruct((n_rows, d), BF16),
        compiler_params=_params(("arbitrary", "arbitrary"), 48),
        name="moe_experts",
    )(blk_exp, blk_rows, n_valid, a_sorted, g_sorted, w_gate, w_up, w_down)


def _moe_combine_kernel(ek_ref, rk_ref, off_ref, np_ref, x_ref, comb_ref, gf_ref, y_hbm, o_ref,
                        posb_ref, sel_ref, ybuf_ref, sem, *, max_pieces, n_exp):
    i = pl.program_id(0)
    tm = x_ref.shape[0]
    n_pieces = np_ref[i]

    def copy(k):
        src = pl.ds(pl.multiple_of(off_ref[i * max_pieces + k], MOE_SEG_ALIGN), MOE_CHUNK)
        return pltpu.make_async_copy(y_hbm.at[src], ybuf_ref.at[pl.ds(k * MOE_CHUNK, MOE_CHUNK)], sem.at[0])

    for k in range(max_pieces):
        @pl.when(k < n_pieces)
        def _():
            copy(k).start()

        @pl.when(k >= n_pieces)
        def _():
            ybuf_ref[k * MOE_CHUNK:(k + 1) * MOE_CHUNK, :] = jnp.zeros((MOE_CHUNK, ybuf_ref.shape[1]), BF16)

    mask = jnp.where(comb_ref[...] > 0.0, 1.0, 0.0)
    r0 = lax.broadcasted_iota(jnp.int32, (tm, tm), 0)
    r1 = lax.broadcasted_iota(jnp.int32, (tm, tm), 1)
    before = jnp.where(r1 < r0, 1.0, 0.0).astype(BF16)
    pos = jnp.where(mask > 0.0, _dot(before, mask.astype(BF16)), -1.0)
    for e in range(n_exp):
        posb_ref[e] = jnp.broadcast_to(pos[:, e:e + 1], (tm, LANES))
    slot = lax.broadcasted_iota(jnp.int32, (tm, MOE_CHUNK), 1).astype(F32)
    for k in range(max_pieces):
        cols = slice(k * MOE_CHUNK, (k + 1) * MOE_CHUNK)
        e_k = ek_ref[i * max_pieces + k]
        r_k = rk_ref[i * max_pieces + k]

        @pl.when(k < n_pieces)
        def _():
            want = slot + (r_k * MOE_CHUNK).astype(F32)
            sel_ref[:, cols] = jnp.where(posb_ref[e_k] == want, 1.0, 0.0).astype(BF16)

        @pl.when(k >= n_pieces)
        def _():
            sel_ref[:, cols] = jnp.zeros((tm, MOE_CHUNK), BF16)

    for k in range(max_pieces):
        @pl.when(k < n_pieces)
        def _():
            copy(k).wait()

    y = x_ref[...] + _dot(sel_ref[...], ybuf_ref[...])
    o_ref[...] = _rms(y, gf_ref[...])


def moe_combine(plan, x, comb, g_final, y_sorted, n_exp, tm, max_pieces):
    ek, rk, off, n_pieces = plan
    n, d = x.shape
    return pl.pallas_call(
        functools.partial(_moe_combine_kernel, max_pieces=max_pieces, n_exp=n_exp),
        grid_spec=pltpu.PrefetchScalarGridSpec(
            num_scalar_prefetch=4,
            grid=(n // tm,),
            in_specs=[pl.BlockSpec((tm, d), lambda i, *_: (i, 0)),
                      pl.BlockSpec((tm, LANES), lambda i, *_: (i, 0)),
                      pl.BlockSpec((1, d), lambda i, *_: (0, 0)),
                      pl.BlockSpec(memory_space=pl.ANY)],
            out_specs=pl.BlockSpec((tm, d), lambda i, *_: (i, 0)),
            scratch_shapes=[pltpu.VMEM((n_exp, tm, LANES), F32),
                            pltpu.VMEM((tm, max_pieces * MOE_CHUNK), BF16),
                            pltpu.VMEM((max_pieces * MOE_CHUNK, d), BF16),
                            pltpu.SemaphoreType.DMA((1,))]),
        out_shape=jax.ShapeDtypeStruct((n, d), F32),
        compiler_params=_params(("arbitrary",), 56),
        name="moe_combine",
    )(ek, rk, off, n_pieces, x, comb, g_final.reshape(1, d), y_sorted)


def moe(x, g, r_hi, r_lo, w_gate, w_up, w_down, g_final, tm=512):
    n, d = x.shape
    n_exp = w_gate.shape[0]
    tm = min(tm, n // 2)
    nt = n // tm
    max_pieces = TOP_K * tm // MOE_CHUNK + n_exp
    n_rows = TOP_K * n + nt * n_exp * (MOE_SEG_ALIGN - 1) + n_exp * (MOE_CHUNK + MOE_ROW_BLOCK)
    n_rows = (n_rows + MOE_ROW_BLOCK - 1) // MOE_ROW_BLOCK * MOE_ROW_BLOCK
    h, comb, cnt = moe_route(x, g, r_hi, r_lo, n_exp, tm)
    cnt = cnt[:, 0, :n_exp].astype(jnp.int32)
    ek, rk, off, n_pieces, blk_exp, blk_rows, n_valid = _moe_plan(cnt, n_rows, max_pieces)
    plan = (ek, rk, off, n_pieces)
    a_sorted, g_sorted = moe_compact(plan, h, comb, n_rows, tm, max_pieces)
    y_sorted = moe_experts(blk_exp, blk_rows, n_valid, a_sorted, g_sorted, w_gate, w_up, w_down)
    return moe_combine(plan, x, comb, g_final, y_sorted, n_exp, tm, max_pieces)


def _pad_cols(w, width):
    return jnp.pad(w, ((0, 0), (0, width - w.shape[1])))


def _even_layer(x2d, bsz, seq, rel_bias, norm_mix, w_in, conv_w, norm_cq, norm_ckv, w_uq, w_uk, w_uv, w_qidx,
                w_out, norm_ffn, ffn_gate, ffn_up, ffn_down):
    nh = w_uq.shape[1]
    a_width = conv_w.shape[1]
    c0 = 3 * a_width
    cq_col, ckv_col = c0, c0 + Q_RANK
    kidx_col = ckv_col + KV_RANK
    widx_col = kidx_col + LANES
    w_in_p = jnp.concatenate([
        w_in[:, :kidx_col],
        _pad_cols(w_in[:, kidx_col:kidx_col + IDX_DIM], LANES),
        _pad_cols(w_in[:, kidx_col + IDX_DIM:], LANES)], axis=1).astype(BF16)
    proj = norm_matmul(x2d, norm_mix, w_in_p).reshape(bsz, seq, -1)

    y_a = conv_mixer(proj, conv_w, a_width)

    w_ukt = jnp.transpose(w_uk, (1, 2, 0)).astype(BF16)
    w_qidx_pad = jnp.pad(w_qidx, ((0, 0), (0, 0), (0, LANES - IDX_DIM))).reshape(Q_RANK, IDX_HEADS * LANES)
    qlat, qidx, ckv, ckvt, wt = dsa_prep(proj, norm_cq, norm_ckv,
                                         w_uq.reshape(Q_RANK, nh * HEAD_DIM).astype(BF16), w_ukt,
                                         w_qidx_pad.astype(BF16), cq_col, ckv_col, widx_col)
    w_uvt = jnp.transpose(w_uv, (1, 2, 0)).astype(BF16)
    y_b = dsa_attention(qlat, qidx, wt, proj, ckv, ckvt, _toeplitz_bias(rel_bias), w_uvt, kidx_col,
                        topk=min(TOPK_MAX, seq // 4))

    n = bsz * seq
    x2d = out_proj(y_a.reshape(n, -1), y_b.reshape(n, -1), w_out.astype(BF16), x2d)
    return ffn(x2d, norm_ffn, ffn_gate.astype(BF16), ffn_up.astype(BF16), ffn_down.astype(BF16))


def _odd_layer(x2d, bsz, seq, norm_mix, w_in, norm_v, w_s, b_s, w_out, norm_ffn, router, exp_gate, exp_up,
               exp_down, final_norm):
    c_width = norm_v.shape[0]
    nh = (w_in.shape[1] - 2 * c_width) // (3 * HEAD_DIM)
    d_width = nh * HEAD_DIM
    col = jnp.arange(w_in.shape[1])
    is_q = (col >= 2 * c_width) & (col < 2 * c_width + d_width)
    w_in_s = jnp.where(is_q[None, :], w_in * (HEAD_DIM ** -0.5), w_in).astype(BF16)
    proj = norm_matmul(x2d, norm_mix, w_in_s)
    y_c = gmlp_mixer(proj, norm_v, w_s, jnp.transpose(b_s), c_width)
    y_d = stick_breaking(proj.reshape(bsz, seq, -1), nh, 2 * c_width, 2 * c_width + d_width,
                         2 * c_width + 2 * d_width)
    x2d = out_proj(y_c, y_d.reshape(bsz * seq, -1), w_out.astype(BF16), x2d)
    r_pad = _pad_cols(router, LANES)
    r_hi = r_pad.astype(BF16)
    r_lo = (r_pad - r_hi.astype(F32)).astype(BF16)
    return moe(x2d, norm_ffn, r_hi, r_lo, exp_gate.astype(BF16), exp_up.astype(BF16), exp_down.astype(BF16),
               final_norm)


def kernel(x, rel_bias, final_norm, e_norm_mix, e_w_in, e_conv_w, e_norm_cq, e_norm_ckv, e_w_uq, e_w_uk, e_w_uv, e_w_qidx, e_w_out, e_norm_ffn, e_ffn_gate, e_ffn_up, e_ffn_down, o_norm_mix, o_w_in, o_norm_v, o_w_s, o_b_s, o_w_out, o_norm_ffn, o_router, o_exp_gate, o_exp_up, o_exp_down):
    bsz, seq, d = x.shape
    assert e_norm_mix.shape[0] == 1 and o_norm_mix.shape[0] == 1, "one even and one odd layer"
    x2d = x.reshape(bsz * seq, d)
    x2d = _even_layer(x2d, bsz, seq, rel_bias, e_norm_mix[0], e_w_in[0], e_conv_w[0], e_norm_cq[0],
                      e_norm_ckv[0], e_w_uq[0], e_w_uk[0], e_w_uv[0], e_w_qidx[0], e_w_out[0], e_norm_ffn[0],
                      e_ffn_gate[0], e_ffn_up[0], e_ffn_down[0])
    out = _odd_layer(x2d, bsz, seq, o_norm_mix[0], o_w_in[0], o_norm_v[0], o_w_s[0], o_b_s[0], o_w_out[0],
                     o_norm_ffn[0], o_router[0], o_exp_gate[0], o_exp_up[0], o_exp_down[0], final_norm)
    return out.reshape(bsz, seq, d)
```

```python
import functools
import math

import jax
import jax.numpy as jnp
from jax import lax
from jax.experimental import pallas as pl
from jax.experimental.pallas import tpu as pltpu

F32 = jnp.float32
BF16 = jnp.bfloat16

EPS = 1e-6
LANES = 128
COUNT_ROWS = 64
HEAD_DIM = 128
Q_RANK = 512
KV_RANK = 256
IDX_HEADS = 16
IDX_DIM = 64
IDX_SCALE = (IDX_DIM ** -0.5) * (IDX_HEADS ** -0.5)
TOPK_MAX = 256
DSA_KEY_CHUNK = 256
REL_BUCKETS = 32
REL_MAX_DIST = 128
CONV_WIDTH = 3
CHUNK = 128
TOP_K = 2
INT_MIN = -(2 ** 31)
MIB = 1024 * 1024


def _params(sem, vmem_mib):
    return pltpu.CompilerParams(dimension_semantics=sem, vmem_limit_bytes=vmem_mib * MIB)


def _rms(x, g):
    return x * lax.rsqrt(jnp.mean(x * x, axis=-1, keepdims=True) + EPS) * g


def _dot(a, b):
    return jnp.dot(a, b, preferred_element_type=F32)


def _dot_nt(a, b):
    return lax.dot_general(a, b, (((1,), (1,)), ((), ())), preferred_element_type=F32)


def _norm_matmul_kernel(x_ref, g_ref, w_ref, o_ref, h_ref):
    @pl.when(pl.program_id(1) == 0)
    def _():
        h_ref[...] = _rms(x_ref[...], g_ref[...]).astype(BF16)

    o_ref[...] = _dot(h_ref[...], w_ref[...]).astype(o_ref.dtype)


def norm_matmul(x, g, w, tm=1024, tn=1024):
    n, d = x.shape
    nout = w.shape[1]
    return pl.pallas_call(
        _norm_matmul_kernel,
        grid=(n // tm, nout // tn),
        in_specs=[pl.BlockSpec((tm, d), lambda i, j: (i, 0)),
                  pl.BlockSpec((1, d), lambda i, j: (0, 0)),
                  pl.BlockSpec((d, tn), lambda i, j: (0, j))],
        out_specs=pl.BlockSpec((tm, tn), lambda i, j: (i, j)),
        out_shape=jax.ShapeDtypeStruct((n, nout), BF16),
        scratch_shapes=[pltpu.VMEM((tm, d), BF16)],
        compiler_params=_params(("parallel", "arbitrary"), 40),
        name="norm_matmul",
    )(x, g.reshape(1, d), w)


def _conv_kernel(b_ref, c_ref, x_ref, w_ref, o_ref):
    z = c_ref[0].astype(F32) * x_ref[0].astype(F32)
    row = lax.broadcasted_iota(jnp.int32, z.shape, 0)
    y = w_ref[CONV_WIDTH - 1:CONV_WIDTH, :] * z
    for lag in range(1, CONV_WIDTH):
        zl = jnp.where(row >= lag, pltpu.roll(z, lag, 0), 0.0)
        y = y + w_ref[CONV_WIDTH - 1 - lag:CONV_WIDTH - lag, :] * zl
    o_ref[0] = (b_ref[0].astype(F32) * y).astype(o_ref.dtype)


def conv_mixer(proj, conv_w, a_width, tc=256):
    bsz, seq, _ = proj.shape
    nb = a_width // tc
    return pl.pallas_call(
        _conv_kernel,
        grid=(bsz, nb),
        in_specs=[pl.BlockSpec((1, seq, tc), lambda b, c: (b, 0, c)),
                  pl.BlockSpec((1, seq, tc), lambda b, c: (b, 0, nb + c)),
                  pl.BlockSpec((1, seq, tc), lambda b, c: (b, 0, 2 * nb + c)),
                  pl.BlockSpec((CONV_WIDTH, tc), lambda b, c: (0, c))],
        out_specs=pl.BlockSpec((1, seq, tc), lambda b, c: (b, 0, c)),
        out_shape=jax.ShapeDtypeStruct((bsz, seq, a_width), BF16),
        compiler_params=_params(("parallel", "parallel"), 40),
        name="conv_mixer",
    )(proj, proj, proj, conv_w)


def _dsa_prep_kernel(cq_ref, ckv_ref, w_ref, gq_ref, gkv_ref, wuq_ref, wukt_ref, wqi_ref,
                     qlat_ref, qidx_ref, ckvo_ref, ckvt_ref, wt_ref):
    tm = cq_ref.shape[1]
    cqn = _rms(cq_ref[0].astype(F32), gq_ref[...]).astype(BF16)
    ckvn = _rms(ckv_ref[0].astype(F32), gkv_ref[...])
    ckvo_ref[0] = ckvn.astype(BF16)
    for c in range(tm // DSA_KEY_CHUNK):
        rows = slice(c * DSA_KEY_CHUNK, (c + 1) * DSA_KEY_CHUNK)
        ckvt_ref[0, c] = jnp.transpose(ckvn[rows, :]).astype(BF16)
    wt_ref[0] = jnp.transpose(w_ref[0].astype(F32))[:IDX_HEADS, :] * IDX_SCALE
    q = _dot(cqn, wuq_ref[...]).astype(BF16)
    scale = HEAD_DIM ** -0.5
    for h in range(wukt_ref.shape[0]):
        qh = q[:, h * HEAD_DIM:(h + 1) * HEAD_DIM]
        qlat_ref[0, h] = (_dot(qh, wukt_ref[h]) * scale).astype(BF16)
    qi = _dot(cqn, wqi_ref[...]).astype(BF16)
    for g in range(IDX_HEADS):
        qidx_ref[0, g] = qi[:, g * LANES:(g + 1) * LANES]


def dsa_prep(proj, g_cq, g_ckv, w_uq, w_ukt, w_qidx_pad, cq_col, ckv_col, widx_col, tm=256):
    bsz, seq, _ = proj.shape
    nh = w_ukt.shape[0]
    full = lambda *shape: pl.BlockSpec(shape, lambda b, i: (0,) * len(shape))
    return pl.pallas_call(
        _dsa_prep_kernel,
        grid=(bsz, seq // tm),
        in_specs=[pl.BlockSpec((1, tm, Q_RANK), lambda b, i: (b, i, cq_col // Q_RANK)),
                  pl.BlockSpec((1, tm, KV_RANK), lambda b, i: (b, i, ckv_col // KV_RANK)),
                  pl.BlockSpec((1, tm, LANES), lambda b, i: (b, i, widx_col // LANES)),
                  full(1, Q_RANK), full(1, KV_RANK),
                  full(Q_RANK, nh * HEAD_DIM), full(nh, HEAD_DIM, KV_RANK),
                  full(Q_RANK, IDX_HEADS * LANES)],
        out_specs=[pl.BlockSpec((1, nh, tm, KV_RANK), lambda b, i: (b, 0, i, 0)),
                   pl.BlockSpec((1, IDX_HEADS, tm, LANES), lambda b, i: (b, 0, i, 0)),
                   pl.BlockSpec((1, tm, KV_RANK), lambda b, i: (b, i, 0)),
                   pl.BlockSpec((1, tm // DSA_KEY_CHUNK, KV_RANK, DSA_KEY_CHUNK), lambda b, i: (b, i, 0, 0)),
                   pl.BlockSpec((1, IDX_HEADS, tm), lambda b, i: (b, 0, i))],
        out_shape=[jax.ShapeDtypeStruct((bsz, nh, seq, KV_RANK), BF16),
                   jax.ShapeDtypeStruct((bsz, IDX_HEADS, seq, LANES), BF16),
                   jax.ShapeDtypeStruct((bsz, seq, KV_RANK), BF16),
                   jax.ShapeDtypeStruct((bsz, seq // DSA_KEY_CHUNK, KV_RANK, DSA_KEY_CHUNK), BF16),
                   jax.ShapeDtypeStruct((bsz, IDX_HEADS, seq), F32)],
        compiler_params=_params(("parallel", "parallel"), 40),
        name="dsa_prep",
    )(proj, proj, proj, g_cq.reshape(1, -1), g_ckv.reshape(1, -1), w_uq, w_ukt, w_qidx_pad)


def _dsa_kernel(qlat_ref, qidx_ref, wt_ref, kidx_ref, ckv_ref, ckvt_ref, toep_ref, wuvt_ref, o_ref,
                sc_ref, key_ref, p_ref, m_ref, l_ref, alpha_ref, acc_ref, *, topk):
    tq = LANES
    tk = DSA_KEY_CHUNK
    nh = qlat_ref.shape[1]
    i = pl.program_id(1)
    n_chunks = (i * tq) // tk + 1
    kpos = lax.broadcasted_iota(jnp.int32, (tk, tq), 0)
    qpos = lax.broadcasted_iota(jnp.int32, (tk, tq), 1)
    wt = wt_ref[0]
    qi = qidx_ref[0].reshape(IDX_HEADS * tq, LANES)

    def score_body(c, carry):
        kj = kidx_ref[0, pl.ds(pl.multiple_of(c * tk, tk), tk), :]
        sg = _dot_nt(kj, qi)
        s = jnp.zeros((tk, tq), F32)
        for g in range(IDX_HEADS):
            s = s + wt[g:g + 1, :] * jnp.maximum(sg[:, g * tq:(g + 1) * tq], 0.0)
        sc_ref[c] = s
        bits = pltpu.bitcast(s, jnp.int32)
        key = bits ^ ((bits >> 31) & 0x7FFFFFFF)
        causal = (c * tk + kpos) <= (i * tq + qpos)
        key_ref[c] = jnp.where(causal, key, INT_MIN)
        return carry

    lax.fori_loop(0, n_chunks, score_body, 0)

    kvec = jnp.minimum(topk, i * tq + qpos[:1, :] + 1).astype(F32)

    def count_ge(cand):
        def body(c, acc):
            hit = jnp.where(key_ref[c] >= cand, 1.0, 0.0)
            return acc + jnp.sum(hit.reshape(tk // COUNT_ROWS, COUNT_ROWS, tq), axis=0)
        acc = lax.fori_loop(0, n_chunks, body, jnp.zeros((COUNT_ROWS, tq), F32))
        return jnp.sum(acc, axis=0, keepdims=True)

    thr = jnp.where(count_ge(jnp.zeros((1, tq), jnp.int32)) >= kvec, 0, INT_MIN).astype(jnp.int32)

    def bit_body(b, thr):
        cand = thr | jnp.left_shift(jnp.int32(1), 30 - b)
        return jnp.where(count_ge(cand) >= kvec, cand, thr)

    thr = lax.fori_loop(0, 31, bit_body, thr)

    m_ref[...] = jnp.full(m_ref.shape, -jnp.inf, F32)
    l_ref[...] = jnp.zeros(l_ref.shape, F32)
    acc_ref[...] = jnp.zeros(acc_ref.shape, F32)
    q = qlat_ref[0].reshape(nh * tq, KV_RANK)

    def att_body(c, carry):
        cj = ckv_ref[0, pl.ds(pl.multiple_of(c * tk, tk), tk), :]
        sel = key_ref[c] >= thr
        sj = sc_ref[c]
        mdist = [jnp.clip(i - (c * (tk // tq) + r), 0, 2) for r in range(tk // tq)]
        lg = _dot_nt(cj, q)
        for h in range(nh):
            cols = slice(h * tq, (h + 1) * tq)
            bias = jnp.concatenate([toep_ref[h, m] for m in mdist], axis=0)
            lh = jnp.where(sel, lg[:, cols] + bias + sj, -jnp.inf)
            m_old = m_ref[:, cols]
            m_new = jnp.maximum(m_old, jnp.max(lh, axis=0, keepdims=True))
            m_safe = jnp.where(m_new == -jnp.inf, 0.0, m_new)
            alpha = jnp.exp(m_old - m_safe)
            p = jnp.exp(lh - m_safe)
            m_ref[:, cols] = m_new
            l_ref[:, cols] = alpha * l_ref[:, cols] + jnp.sum(p, axis=0, keepdims=True)
            alpha_ref[:, cols] = alpha
            p_ref[:, cols] = p.astype(BF16)
        acc_ref[...] = alpha_ref[...] * acc_ref[...] + _dot(ckvt_ref[0, c], p_ref[...])
        return carry

    lax.fori_loop(0, n_chunks, att_body, 0)

    for h in range(nh):
        cols = slice(h * tq, (h + 1) * tq)
        o_lat_t = (acc_ref[:, cols] / l_ref[:, cols]).astype(BF16)
        y_t = _dot(wuvt_ref[h], o_lat_t)
        o_ref[0, :, h * HEAD_DIM:(h + 1) * HEAD_DIM] = jnp.transpose(y_t).astype(o_ref.dtype)


def dsa_attention(qlat, qidx, wt, proj, ckv, ckvt, toep, w_uvt, kidx_col, topk):
    bsz, nh, seq, _ = qlat.shape
    tq = LANES
    tk = DSA_KEY_CHUNK
    nkb = seq // tk
    full = lambda *shape: pl.BlockSpec(shape, lambda b, i: (0,) * len(shape))
    return pl.pallas_call(
        functools.partial(_dsa_kernel, topk=topk),
        grid=(bsz, seq // tq),
        in_specs=[pl.BlockSpec((1, nh, tq, KV_RANK), lambda b, i: (b, 0, i, 0)),
                  pl.BlockSpec((1, IDX_HEADS, tq, LANES), lambda b, i: (b, 0, i, 0)),
                  pl.BlockSpec((1, IDX_HEADS, tq), lambda b, i: (b, 0, i)),
                  pl.BlockSpec((1, seq, LANES), lambda b, i: (b, 0, kidx_col // LANES)),
                  pl.BlockSpec((1, seq, KV_RANK), lambda b, i: (b, 0, 0)),
                  pl.BlockSpec((1, nkb, KV_RANK, tk), lambda b, i: (b, 0, 0, 0)),
                  full(nh, 3, tq, tq),
                  full(nh, HEAD_DIM, KV_RANK)],
        out_specs=pl.BlockSpec((1, tq, nh * HEAD_DIM), lambda b, i: (b, i, 0)),
        out_shape=jax.ShapeDtypeStruct((bsz, seq, nh * HEAD_DIM), BF16),
        scratch_shapes=[pltpu.VMEM((nkb, tk, tq), F32),
                        pltpu.VMEM((nkb, tk, tq), jnp.int32),
                        pltpu.VMEM((tk, nh * tq), BF16),
                        pltpu.VMEM((1, nh * tq), F32),
                        pltpu.VMEM((1, nh * tq), F32),
                        pltpu.VMEM((1, nh * tq), F32),
                        pltpu.VMEM((KV_RANK, nh * tq), F32)],
        compiler_params=_params(("parallel", "arbitrary"), 40),
        name="dsa_attention",
    )(qlat, qidx, wt, proj, ckv, ckvt, toep, w_uvt)


def _rel_bucket(dist):
    n = jnp.maximum(dist, 0)
    max_exact = REL_BUCKETS // 2
    nf = jnp.maximum(n, max_exact).astype(F32)
    large = max_exact + (jnp.log(nf / max_exact) / math.log(REL_MAX_DIST / max_exact)
                         * (REL_BUCKETS - max_exact)).astype(jnp.int32)
    large = jnp.minimum(large, REL_BUCKETS - 1)
    return jnp.where(n < max_exact, n, large)


def _toeplitz_bias(rel_bias):
    t = LANES
    r = jnp.arange(t, dtype=jnp.int32)
    tiles = []
    for m in range(3):
        dist = m * t + r[None, :] - r[:, None] if m < 2 else jnp.full((t, t), 2 * REL_MAX_DIST, jnp.int32)
        tiles.append(_rel_bucket(dist))
    bucket = jnp.stack(tiles)
    bias = jnp.zeros((rel_bias.shape[1],) + bucket.shape, F32)
    for k in range(REL_BUCKETS):
        bias = jnp.where(bucket[None] == k, rel_bias[k][:, None, None, None], bias)
    return bias


def _out_proj_kernel(ya_ref, yb_ref, wa_ref, wb_ref, x_ref, o_ref):
    o_ref[...] = x_ref[...] + _dot(ya_ref[...], wa_ref[...]) + _dot(yb_ref[...], wb_ref[...])


def out_proj(ya, yb, w_out, x, tm=1024, tn=1024):
    n, ka = ya.shape
    kb = yb.shape[1]
    d = x.shape[1]
    wa, wb = w_out[:ka], w_out[ka:]
    return pl.pallas_call(
        _out_proj_kernel,
        grid=(n // tm, d // tn),
        in_specs=[pl.BlockSpec((tm, ka), lambda i, j: (i, 0)),
                  pl.BlockSpec((tm, kb), lambda i, j: (i, 0)),
                  pl.BlockSpec((ka, tn), lambda i, j: (0, j)),
                  pl.BlockSpec((kb, tn), lambda i, j: (0, j)),
                  pl.BlockSpec((tm, tn), lambda i, j: (i, j))],
        out_specs=pl.BlockSpec((tm, tn), lambda i, j: (i, j)),
        out_shape=jax.ShapeDtypeStruct((n, d), F32),
        compiler_params=_params(("parallel", "parallel"), 40),
        name="out_proj",
    )(ya, yb, wa, wb, x)


def _silu(a):
    return a / (1.0 + jnp.exp(-a))


def _ffn_kernel(x_ref, g_ref, wg_ref, wu_ref, wd_ref, o_ref, h_ref):
    @pl.when(pl.program_id(1) == 0)
    def _():
        x = x_ref[...]
        h_ref[...] = _rms(x, g_ref[...]).astype(BF16)
        o_ref[...] = x

    h = h_ref[...]
    act = (_silu(_dot(h, wg_ref[...])) * _dot(h, wu_ref[...])).astype(BF16)
    o_ref[...] += _dot(act, wd_ref[...])


def ffn(x, g, w_gate, w_up, w_down, tm=1024, tf=512):
    n, d = x.shape
    dff = w_gate.shape[1]
    return pl.pallas_call(
        _ffn_kernel,
        grid=(n // tm, dff // tf),
        in_specs=[pl.BlockSpec((tm, d), lambda i, f: (i, 0)),
                  pl.BlockSpec((1, d), lambda i, f: (0, 0)),
                  pl.BlockSpec((d, tf), lambda i, f: (0, f)),
                  pl.BlockSpec((d, tf), lambda i, f: (0, f)),
                  pl.BlockSpec((tf, d), lambda i, f: (f, 0))],
        out_specs=pl.BlockSpec((tm, d), lambda i, f: (i, 0)),
        out_shape=jax.ShapeDtypeStruct((n, d), F32),
        scratch_shapes=[pltpu.VMEM((tm, d), BF16)],
        compiler_params=_params(("parallel", "arbitrary"), 58),
        name="ffn",
    )(x, g.reshape(1, d), w_gate, w_up, w_down)


def _gelu(x):
    return x * (0.5 * (1.0 + jnp.tanh(math.sqrt(2.0 / math.pi) * (x + 0.044715 * (x * x * x)))))


def _gmlp_kernel(cu_ref, cv_ref, gv_ref, ws_ref, bst_ref, o_ref):
    tm = cu_ref.shape[0]
    ng = ws_ref.shape[0]
    gw = cu_ref.shape[1] // ng
    u = _gelu(cu_ref[...].astype(F32))
    vn = _rms(_gelu(cv_ref[...].astype(F32)), gv_ref[...]).astype(BF16)
    row = lax.broadcasted_iota(jnp.int32, (CHUNK, CHUNK), 0)
    col = lax.broadcasted_iota(jnp.int32, (CHUNK, CHUNK), 1)
    for g in range(ng):
        wsg = jnp.where(row >= col, ws_ref[g], 0.0).astype(BF16)
        bias = bst_ref[:, g:g + 1]
        cols = slice(g * gw, (g + 1) * gw)
        for c in range(tm // CHUNK):
            rows = slice(c * CHUNK, (c + 1) * CHUNK)
            mixed = _dot(wsg, vn[rows, cols]) + bias
            o_ref[rows, cols] = (u[rows, cols] * mixed).astype(o_ref.dtype)


def gmlp_mixer(proj, g_v, w_s, b_s_t, c_width, tm=256):
    n = proj.shape[0]
    ng = w_s.shape[0]
    return pl.pallas_call(
        _gmlp_kernel,
        grid=(n // tm,),
        in_specs=[pl.BlockSpec((tm, c_width), lambda i: (i, 0)),
                  pl.BlockSpec((tm, c_width), lambda i: (i, 1)),
                  pl.BlockSpec((1, c_width), lambda i: (0, 0)),
                  pl.BlockSpec((ng, CHUNK, CHUNK), lambda i: (0, 0, 0)),
                  pl.BlockSpec((CHUNK, ng), lambda i: (0, 0))],
        out_specs=pl.BlockSpec((tm, c_width), lambda i: (i, 0)),
        out_shape=jax.ShapeDtypeStruct((n, c_width), BF16),
        compiler_params=_params(("parallel",), 40),
        name="gmlp_mixer",
    )(proj, proj, g_v.reshape(1, -1), w_s, b_s_t)


def _sb_kernel(q_ref, k_ref, v_ref, u_ref, o_ref, acc_ref, carry_ref, *, tk):
    tq = q_ref.shape[1]
    i = pl.program_id(2)
    acc_ref[...] = jnp.zeros(acc_ref.shape, F32)
    carry_ref[...] = jnp.zeros(carry_ref.shape, F32)
    q = q_ref[0]
    tpos = i * tq + lax.broadcasted_iota(jnp.int32, (tq, tk), 0)
    col = lax.broadcasted_iota(jnp.int32, (tq, tk), 1)
    n_diag = tq // tk

    def block(j, diagonal):
        start = pl.multiple_of(j * tk, tk)
        for h in range(q_ref.shape[2] // HEAD_DIM):
            cols = slice(h * HEAD_DIM, (h + 1) * HEAD_DIM)
            kj = k_ref[0, pl.ds(start, tk), cols]
            vj = v_ref[0, pl.ds(start, tk), cols]
            z = _dot_nt(q[:, cols], kj)
            log_keep = -(jnp.maximum(z, 0.0) + jnp.log(1.0 + jnp.exp(-jnp.abs(z))))
            log_beta = z + log_keep
            if diagonal:
                past = (j * tk + col) < tpos
                log_keep = jnp.where(past, log_keep, 0.0)
            hi = log_keep.astype(BF16)
            lo = (log_keep - hi.astype(F32)).astype(BF16)
            excl = _dot(hi, u_ref[...]) + _dot(lo, u_ref[...])
            wts = jnp.exp(log_beta + excl + carry_ref[h])
            if diagonal:
                wts = jnp.where(past, wts, 0.0)
            acc_ref[:, cols] += _dot(wts.astype(BF16), vj)
            carry_ref[h] += jnp.sum(log_keep, axis=1, keepdims=True)

    for d in range(n_diag):
        block((i + 1) * n_diag - 1 - d, True)

    def body(step, c):
        block(i * n_diag - 1 - step, False)
        return c

    lax.fori_loop(0, i * n_diag, body, 0)
    o_ref[0] = acc_ref[...].astype(o_ref.dtype)


def stick_breaking(proj, nh, q_col, k_col, v_col, tq=512, tk=256, heads_per_step=2):
    bsz, seq, _ = proj.shape
    tq = min(tq, seq)
    tk = min(tk, tq)
    r = jnp.arange(tk, dtype=jnp.int32)
    upper = (r[:, None] > r[None, :]).astype(BF16)
    hw = heads_per_step * HEAD_DIM
    qb, kb, vb = q_col // hw, k_col // hw, v_col // hw
    assert q_col % hw == 0 and k_col % hw == 0 and v_col % hw == 0 and nh % heads_per_step == 0
    return pl.pallas_call(
        functools.partial(_sb_kernel, tk=tk),
        grid=(bsz, nh // heads_per_step, seq // tq),
        in_specs=[pl.BlockSpec((1, tq, hw), lambda b, h, i: (b, i, qb + h)),
                  pl.BlockSpec((1, seq, hw), lambda b, h, i: (b, 0, kb + h)),
                  pl.BlockSpec((1, seq, hw), lambda b, h, i: (b, 0, vb + h)),
                  pl.BlockSpec((tk, tk), lambda b, h, i: (0, 0))],
        out_specs=pl.BlockSpec((1, tq, hw), lambda b, h, i: (b, i, h)),
        out_shape=jax.ShapeDtypeStruct((bsz, seq, nh * HEAD_DIM), BF16),
        scratch_shapes=[pltpu.VMEM((tq, hw), F32), pltpu.VMEM((heads_per_step, tq, 1), F32)],
        compiler_params=_params(("parallel", "parallel", "arbitrary"), 40),
        name="stick_breaking",
    )(proj, proj, proj, upper)


MOE_CHUNK = 64
MOE_SEG_ALIGN = 16
MOE_ROW_BLOCK = 1024
MOE_SUB_BLOCK = 256


def _moe_route_kernel(x_ref, g_ref, rhi_ref, rlo_ref, h_ref, comb_ref, cnt_ref, *, n_exp):
    tm = x_ref.shape[0]
    h32 = _rms(x_ref[...], g_ref[...])
    hi = h32.astype(BF16)
    lo = (h32 - hi.astype(F32)).astype(BF16)
    h_ref[...] = hi
    logits = _dot(hi, rhi_ref[...]) + (_dot(hi, rlo_ref[...]) + _dot(lo, rhi_ref[...]))
    lane = lax.broadcasted_iota(jnp.int32, (tm, LANES), 1)
    lanef = lane.astype(F32)
    lg = jnp.where(lane < n_exp, logits, -jnp.inf)
    m1 = jnp.max(lg, axis=1, keepdims=True)
    i1 = jnp.min(jnp.where(lg == m1, lanef, float(LANES)), axis=1, keepdims=True)
    lg2 = jnp.where(lanef == i1, -jnp.inf, lg)
    m2 = jnp.max(lg2, axis=1, keepdims=True)
    i2 = jnp.min(jnp.where(lg2 == m2, lanef, float(LANES)), axis=1, keepdims=True)
    e2 = jnp.exp(m2 - m1)
    g1 = 1.0 / (1.0 + e2)
    comb = jnp.where(lanef == i1, g1, 0.0) + jnp.where(lanef == i2, e2 * g1, 0.0)
    comb_ref[...] = comb
    cnt_ref[0] = jnp.sum(jnp.where(comb > 0.0, 1.0, 0.0), axis=0, keepdims=True)


def moe_route(x, g, r_hi, r_lo, n_exp, tm):
    n, d = x.shape
    nt = n // tm
    return pl.pallas_call(
        functools.partial(_moe_route_kernel, n_exp=n_exp),
        grid=(nt,),
        in_specs=[pl.BlockSpec((tm, d), lambda i: (i, 0)),
                  pl.BlockSpec((1, d), lambda i: (0, 0)),
                  pl.BlockSpec((d, LANES), lambda i: (0, 0)),
                  pl.BlockSpec((d, LANES), lambda i: (0, 0))],
        out_specs=[pl.BlockSpec((tm, d), lambda i: (i, 0)),
                   pl.BlockSpec((tm, LANES), lambda i: (i, 0)),
                   pl.BlockSpec((1, 1, LANES), lambda i: (i, 0, 0))],
        out_shape=[jax.ShapeDtypeStruct((n, d), BF16),
                   jax.ShapeDtypeStruct((n, LANES), F32),
                   jax.ShapeDtypeStruct((nt, 1, LANES), F32)],
        compiler_params=_params(("parallel",), 40),
        name="moe_route",
    )(x, g.reshape(1, d), r_hi, r_lo)


def _moe_plan(cnt, n_rows_static, max_pieces):
    nt, n_exp = cnt.shape
    i32 = jnp.int32
    cp = (cnt + MOE_SEG_ALIGN - 1) // MOE_SEG_ALIGN * MOE_SEG_ALIGN
    reg = (jnp.sum(cp, axis=0) + MOE_CHUNK + MOE_ROW_BLOCK - 1) // MOE_ROW_BLOCK * MOE_ROW_BLOCK
    base = jnp.cumsum(reg) - reg
    seg = base[None, :] + jnp.cumsum(cp, axis=0) - cp
    nce = (cnt + MOE_CHUNK - 1) // MOE_CHUNK
    cum = jnp.cumsum(nce, axis=1)
    n_pieces = cum[:, -1]
    k = jnp.arange(max_pieces, dtype=i32)
    ek = jnp.minimum(jnp.sum(cum[:, None, :] <= k[None, :, None], axis=-1), n_exp - 1).astype(i32)
    is_e = ek[:, :, None] == jnp.arange(n_exp, dtype=i32)
    pick = lambda tbl: jnp.sum(jnp.where(is_e, tbl[:, None, :], 0), axis=-1)
    rk = k[None, :] - pick(cum - nce)
    off = pick(seg) + rk * MOE_CHUNK
    valid = k[None, :] < n_pieces[:, None]
    rk = jnp.where(valid, rk, 0).astype(i32)
    off = jnp.where(valid, off, 0).astype(i32)
    n_blocks = n_rows_static // MOE_ROW_BLOCK
    cb = jnp.cumsum(reg // MOE_ROW_BLOCK)
    blk = jnp.arange(n_blocks, dtype=i32)
    blk_exp = jnp.minimum(jnp.sum(cb[None, :] <= blk[:, None], axis=1), n_exp - 1)
    is_be = blk_exp[:, None] == jnp.arange(n_exp, dtype=i32)
    pick_b = lambda vec: jnp.sum(jnp.where(is_be, vec[None, :], 0), axis=-1)
    used = jnp.sum(cp, axis=0)
    blk_rows = jnp.clip(pick_b(used) - (blk * MOE_ROW_BLOCK - pick_b(base)), 0, MOE_ROW_BLOCK)
    return (ek.reshape(-1), rk.reshape(-1), off.reshape(-1), n_pieces.astype(i32),
            blk_exp.astype(i32), blk_rows.astype(i32), cb[-1:].astype(i32))


def _moe_compact_kernel(ek_ref, rk_ref, off_ref, np_ref, h_ref, comb_ref, a_in, g_in, a_out, g_out,
                        pos_ref, p_ref, stage_ref, gst_ref, sem, *, max_pieces):
    del a_in, g_in
    i = pl.program_id(0)
    tm = h_ref.shape[0]
    n_pieces = np_ref[i]
    comb = comb_ref[...]
    mask_t = jnp.transpose(jnp.where(comb > 0.0, 1.0, 0.0))
    r0 = lax.broadcasted_iota(jnp.int32, (tm, tm), 0)
    r1 = lax.broadcasted_iota(jnp.int32, (tm, tm), 1)
    before = jnp.where(r0 < r1, 1.0, 0.0).astype(BF16)
    pos_t = _dot(mask_t.astype(BF16), before)
    pos_ref[...] = jnp.where(mask_t > 0.0, pos_t, -1.0)
    slot = lax.broadcasted_iota(jnp.int32, (MOE_CHUNK, tm), 0).astype(F32)
    for k in range(max_pieces):
        rows = slice(k * MOE_CHUNK, (k + 1) * MOE_CHUNK)
        e_k = ek_ref[i * max_pieces + k]
        r_k = rk_ref[i * max_pieces + k]

        @pl.when(k < n_pieces)
        def _():
            want = slot + (r_k * MOE_CHUNK).astype(F32)
            p_ref[rows] = jnp.where(pos_ref[pl.ds(e_k, 1), :] == want, 1.0, 0.0).astype(BF16)

        @pl.when(k >= n_pieces)
        def _():
            p_ref[rows] = jnp.zeros((MOE_CHUNK, tm), BF16)

    d = h_ref.shape[1]
    nc = 512
    for c in range(d // nc):
        stage_ref[:, c * nc:(c + 1) * nc] = _dot(p_ref[...], h_ref[:, c * nc:(c + 1) * nc]).astype(BF16)
    c_hi = comb.astype(BF16)
    c_mid = (comb - c_hi.astype(F32)).astype(BF16)
    c_lo = (comb - c_hi.astype(F32) - c_mid.astype(F32)).astype(BF16)
    p = p_ref[...]
    gst_ref[...] = _dot(p, c_hi) + _dot(p, c_mid) + _dot(p, c_lo)

    def copies(k):
        rows = pl.ds(k * MOE_CHUNK, MOE_CHUNK)
        dst = pl.ds(pl.multiple_of(off_ref[i * max_pieces + k], MOE_SEG_ALIGN), MOE_CHUNK)
        return (pltpu.make_async_copy(stage_ref.at[rows], a_out.at[dst], sem.at[0]),
                pltpu.make_async_copy(gst_ref.at[rows], g_out.at[dst], sem.at[1]))

    for k in range(max_pieces):
        @pl.when(k < n_pieces)
        def _():
            for cp in copies(k):
                cp.start()
    for k in range(max_pieces):
        @pl.when(k < n_pieces)
        def _():
            for cp in copies(k):
                cp.wait()


def moe_compact(plan, h, comb, n_rows, tm, max_pieces):
    ek, rk, off, n_pieces = plan
    n, d = h.shape
    a0 = jnp.zeros((n_rows, d), BF16)
    g0 = jnp.zeros((n_rows, LANES), F32)
    any_spec = pl.BlockSpec(memory_space=pl.ANY)
    return pl.pallas_call(
        functools.partial(_moe_compact_kernel, max_pieces=max_pieces),
        grid_spec=pltpu.PrefetchScalarGridSpec(
            num_scalar_prefetch=4,
            grid=(n // tm,),
            in_specs=[pl.BlockSpec((tm, d), lambda i, *_: (i, 0)),
                      pl.BlockSpec((tm, LANES), lambda i, *_: (i, 0)),
                      any_spec, any_spec],
            out_specs=[any_spec, any_spec],
            scratch_shapes=[pltpu.VMEM((LANES, tm), F32),
                            pltpu.VMEM((max_pieces * MOE_CHUNK, tm), BF16),
                            pltpu.VMEM((max_pieces * MOE_CHUNK, d), BF16),
                            pltpu.VMEM((max_pieces * MOE_CHUNK, LANES), F32),
                            pltpu.SemaphoreType.DMA((2,))]),
        out_shape=[jax.ShapeDtypeStruct((n_rows, d), BF16), jax.ShapeDtypeStruct((n_rows, LANES), F32)],
        input_output_aliases={6: 0, 7: 1},
        compiler_params=_params(("arbitrary",), 48),
        name="moe_compact",
    )(ek, rk, off, n_pieces, h, comb, a0, g0)


def _moe_expert_kernel(be_ref, br_ref, nv_ref, a_ref, gs_ref, wg_ref, wu_ref, wd_ref, y_ref, acc_ref):
    b = pl.program_id(0)
    f = pl.program_id(1)
    rb = a_ref.shape[0]
    n_rows = br_ref[b]
    expert = be_ref[b]
    busy = b < nv_ref[0]

    def step(rows):
        @pl.when(f == 0)
        def _():
            acc_ref[rows] = jnp.zeros((rows.stop - rows.start, acc_ref.shape[1]), F32)

        a = a_ref[rows]
        act = (_silu(_dot(a, wg_ref[0].astype(BF16))) * _dot(a, wu_ref[0].astype(BF16))).astype(BF16)
        acc_ref[rows] += _dot(act, wd_ref[0].astype(BF16))

        @pl.when(f == pl.num_programs(1) - 1)
        def _():
            gs = gs_ref[rows]
            lane = lax.broadcasted_iota(jnp.int32, gs.shape, 1)
            gate = jnp.sum(jnp.where(lane == expert, gs, 0.0), axis=1, keepdims=True)
            y_ref[rows] = (acc_ref[rows] * gate).astype(y_ref.dtype)

    @pl.when(busy & (n_rows == rb))
    def _():
        step(slice(0, rb))

    for sb in range(rb // MOE_SUB_BLOCK):
        rows = slice(sb * MOE_SUB_BLOCK, (sb + 1) * MOE_SUB_BLOCK)

        @pl.when(busy & (n_rows < rb) & (sb * MOE_SUB_BLOCK < n_rows))
        def _():
            step(rows)

        @pl.when((jnp.logical_not(busy) | (sb * MOE_SUB_BLOCK >= n_rows)) & (f == 0))
        def _():
            y_ref[rows] = jnp.zeros((MOE_SUB_BLOCK, y_ref.shape[1]), y_ref.dtype)


def moe_experts(blk_exp, blk_rows, n_valid, a_sorted, g_sorted, w_gate, w_up, w_down, tf=512):
    n_rows, d = a_sorted.shape
    n_exp, _, dff = w_gate.shape
    nf = dff // tf
    rb = MOE_ROW_BLOCK

    def blk(b, nv):
        return jnp.minimum(b, nv[0] - 1)

    def fblk(b, f, nv):
        return jnp.where(b < nv[0], f, nf - 1)

    return pl.pallas_call(
        _moe_expert_kernel,
        grid_spec=pltpu.PrefetchScalarGridSpec(
            num_scalar_prefetch=3,
            grid=(n_rows // rb, nf),
            in_specs=[pl.BlockSpec((rb, d), lambda b, f, be, br, nv: (blk(b, nv), 0)),
                      pl.BlockSpec((rb, LANES), lambda b, f, be, br, nv: (blk(b, nv), 0)),
                      pl.BlockSpec((1, d, tf), lambda b, f, be, br, nv: (be[blk(b, nv)], 0, fblk(b, f, nv))),
                      pl.BlockSpec((1, d, tf), lambda b, f, be, br, nv: (be[blk(b, nv)], 0, fblk(b, f, nv))),
                      pl.BlockSpec((1, tf, d), lambda b, f, be, br, nv: (be[blk(b, nv)], fblk(b, f, nv), 0))],
            out_specs=pl.BlockSpec((rb, d), lambda b, f, be, br, nv: (b, 0)),
            scratch_shapes=[pltpu.VMEM((rb, d), F32)]),
        out_shape=jax.ShapeDtypeStruct((n_rows, d), BF16),
        compiler_params=_params(("arbitrary", "arbitrary"), 58),
        name="moe_experts",
    )(blk_exp, blk_rows, n_valid, a_sorted, g_sorted, w_gate, w_up, w_down)


def _moe_combine_kernel(ek_ref, rk_ref, off_ref, np_ref, x_ref, comb_ref, gf_ref, y_hbm, o_ref,
                        posb_ref, sel_ref, ybuf_ref, sem, *, max_pieces, n_exp):
    i = pl.program_id(0)
    tm = x_ref.shape[0]
    n_pieces = np_ref[i]

    def copy(k):
        src = pl.ds(pl.multiple_of(off_ref[i * max_pieces + k], MOE_SEG_ALIGN), MOE_CHUNK)
        return pltpu.make_async_copy(y_hbm.at[src], ybuf_ref.at[pl.ds(k * MOE_CHUNK, MOE_CHUNK)], sem.at[0])

    for k in range(max_pieces):
        @pl.when(k < n_pieces)
        def _():
            copy(k).start()

        @pl.when(k >= n_pieces)
        def _():
            ybuf_ref[k * MOE_CHUNK:(k + 1) * MOE_CHUNK, :] = jnp.zeros((MOE_CHUNK, ybuf_ref.shape[1]), BF16)

    mask = jnp.where(comb_ref[...] > 0.0, 1.0, 0.0)
    r0 = lax.broadcasted_iota(jnp.int32, (tm, tm), 0)
    r1 = lax.broadcasted_iota(jnp.int32, (tm, tm), 1)
    before = jnp.where(r1 < r0, 1.0, 0.0).astype(BF16)
    pos = jnp.where(mask > 0.0, _dot(before, mask.astype(BF16)), -1.0)
    for e in range(n_exp):
        posb_ref[e] = jnp.broadcast_to(pos[:, e:e + 1], (tm, LANES))
    per_tile = LANES // MOE_CHUNK
    lane = lax.broadcasted_iota(jnp.int32, (tm, LANES), 1)
    for kk in range(max_pieces // per_tile):
        hit = None
        for sub in range(per_tile):
            k = kk * per_tile + sub
            e_k = ek_ref[i * max_pieces + k]
            first = jnp.where(k < n_pieces, rk_ref[i * max_pieces + k] * MOE_CHUNK - sub * MOE_CHUNK, -2 - LANES)
            here = (lane >= sub * MOE_CHUNK) & (lane < (sub + 1) * MOE_CHUNK)
            match = here & (posb_ref[e_k] == (lane + first).astype(F32))
            hit = match if hit is None else (hit | match)
        sel_ref[:, kk * LANES:(kk + 1) * LANES] = jnp.where(hit, 1.0, 0.0).astype(BF16)

    for k in range(max_pieces):
        @pl.when(k < n_pieces)
        def _():
            copy(k).wait()

    y = x_ref[...] + _dot(sel_ref[...], ybuf_ref[...])
    o_ref[...] = _rms(y, gf_ref[...])


def moe_combine(plan, x, comb, g_final, y_sorted, n_exp, tm, max_pieces):
    ek, rk, off, n_pieces = plan
    n, d = x.shape
    return pl.pallas_call(
        functools.partial(_moe_combine_kernel, max_pieces=max_pieces, n_exp=n_exp),
        grid_spec=pltpu.PrefetchScalarGridSpec(
            num_scalar_prefetch=4,
            grid=(n // tm,),
            in_specs=[pl.BlockSpec((tm, d), lambda i, *_: (i, 0)),
                      pl.BlockSpec((tm, LANES), lambda i, *_: (i, 0)),
                      pl.BlockSpec((1, d), lambda i, *_: (0, 0)),
                      pl.BlockSpec(memory_space=pl.ANY)],
            out_specs=pl.BlockSpec((tm, d), lambda i, *_: (i, 0)),
            scratch_shapes=[pltpu.VMEM((n_exp, tm, LANES), F32),
                            pltpu.VMEM((tm, max_pieces * MOE_CHUNK), BF16),
                            pltpu.VMEM((max_pieces * MOE_CHUNK, d), BF16),
                            pltpu.SemaphoreType.DMA((1,))]),
        out_shape=jax.ShapeDtypeStruct((n, d), F32),
        compiler_params=_params(("arbitrary",), 56),
        name="moe_combine",
    )(ek, rk, off, n_pieces, x, comb, g_final.reshape(1, d), y_sorted)


def moe(x, g, r_hi, r_lo, w_gate, w_up, w_down, g_final, tm=512):
    n, d = x.shape
    n_exp = w_gate.shape[0]
    tm = min(tm, n // 2)
    nt = n // tm
    max_pieces = TOP_K * tm // MOE_CHUNK + n_exp
    n_rows = TOP_K * n + nt * n_exp * (MOE_SEG_ALIGN - 1) + n_exp * (MOE_CHUNK + MOE_ROW_BLOCK)
    n_rows = (n_rows + MOE_ROW_BLOCK - 1) // MOE_ROW_BLOCK * MOE_ROW_BLOCK
    h, comb, cnt = moe_route(x, g, r_hi, r_lo, n_exp, tm)
    cnt = cnt[:, 0, :n_exp].astype(jnp.int32)
    ek, rk, off, n_pieces, blk_exp, blk_rows, n_valid = _moe_plan(cnt, n_rows, max_pieces)
    plan = (ek, rk, off, n_pieces)
    a_sorted, g_sorted = moe_compact(plan, h, comb, n_rows, tm, max_pieces)
    y_sorted = moe_experts(blk_exp, blk_rows, n_valid, a_sorted, g_sorted, w_gate, w_up, w_down)
    return moe_combine(plan, x, comb, g_final, y_sorted, n_exp, tm, max_pieces)


def _pad_cols(w, width):
    return jnp.pad(w, ((0, 0), (0, width - w.shape[1])))


def _even_layer(x2d, bsz, seq, rel_bias, norm_mix, w_in, conv_w, norm_cq, norm_ckv, w_uq, w_uk, w_uv, w_qidx,
                w_out, norm_ffn, ffn_gate, ffn_up, ffn_down):
    nh = w_uq.shape[1]
    a_width = conv_w.shape[1]
    c0 = 3 * a_width
    cq_col, ckv_col = c0, c0 + Q_RANK
    kidx_col = ckv_col + KV_RANK
    widx_col = kidx_col + LANES
    w_in_p = jnp.concatenate([
        w_in[:, :kidx_col],
        _pad_cols(w_in[:, kidx_col:kidx_col + IDX_DIM], LANES),
        _pad_cols(w_in[:, kidx_col + IDX_DIM:], LANES)], axis=1).astype(BF16)
    proj = norm_matmul(x2d, norm_mix, w_in_p).reshape(bsz, seq, -1)

    y_a = conv_mixer(proj, conv_w, a_width)

    w_ukt = jnp.transpose(w_uk, (1, 2, 0)).astype(BF16)
    w_qidx_pad = jnp.pad(w_qidx, ((0, 0), (0, 0), (0, LANES - IDX_DIM))).reshape(Q_RANK, IDX_HEADS * LANES)
    qlat, qidx, ckv, ckvt, wt = dsa_prep(proj, norm_cq, norm_ckv,
                                         w_uq.reshape(Q_RANK, nh * HEAD_DIM).astype(BF16), w_ukt,
                                         w_qidx_pad.astype(BF16), cq_col, ckv_col, widx_col)
    w_uvt = jnp.transpose(w_uv, (1, 2, 0)).astype(BF16)
    y_b = dsa_attention(qlat, qidx, wt, proj, ckv, ckvt, _toeplitz_bias(rel_bias), w_uvt, kidx_col,
                        topk=min(TOPK_MAX, seq // 4))

    n = bsz * seq
    x2d = out_proj(y_a.reshape(n, -1), y_b.reshape(n, -1), w_out.astype(BF16), x2d)
    return ffn(x2d, norm_ffn, ffn_gate.astype(BF16), ffn_up.astype(BF16), ffn_down.astype(BF16))


def _odd_layer(x2d, bsz, seq, norm_mix, w_in, norm_v, w_s, b_s, w_out, norm_ffn, router, exp_gate, exp_up,
               exp_down, final_norm):
    c_width = norm_v.shape[0]
    nh = (w_in.shape[1] - 2 * c_width) // (3 * HEAD_DIM)
    d_width = nh * HEAD_DIM
    col = jnp.arange(w_in.shape[1])
    is_q = (col >= 2 * c_width) & (col < 2 * c_width + d_width)
    w_in_s = jnp.where(is_q[None, :], w_in * (HEAD_DIM ** -0.5), w_in).astype(BF16)
    proj = norm_matmul(x2d, norm_mix, w_in_s)
    y_c = gmlp_mixer(proj, norm_v, w_s, jnp.transpose(b_s), c_width)
    y_d = stick_breaking(proj.reshape(bsz, seq, -1), nh, 2 * c_width, 2 * c_width + d_width,
                         2 * c_width + 2 * d_width)
    x2d = out_proj(y_c, y_d.reshape(bsz * seq, -1), w_out.astype(BF16), x2d)
    r_pad = _pad_cols(router, LANES)
    r_hi = r_pad.astype(BF16)
    r_lo = (r_pad - r_hi.astype(F32)).astype(BF16)
    return moe(x2d, norm_ffn, r_hi, r_lo, exp_gate, exp_up, exp_down,
               final_norm)


def kernel(x, rel_bias, final_norm, e_norm_mix, e_w_in, e_conv_w, e_norm_cq, e_norm_ckv, e_w_uq, e_w_uk, e_w_uv, e_w_qidx, e_w_out, e_norm_ffn, e_ffn_gate, e_ffn_up, e_ffn_down, o_norm_mix, o_w_in, o_norm_v, o_w_s, o_b_s, o_w_out, o_norm_ffn, o_router, o_exp_gate, o_exp_up, o_exp_down):
    bsz, seq, d = x.shape
    assert e_norm_mix.shape[0] == 1 and o_norm_mix.shape[0] == 1, "one even and one odd layer"
    x2d = x.reshape(bsz * seq, d)
    x2d = _even_layer(x2d, bsz, seq, rel_bias, e_norm_mix[0], e_w_in[0], e_conv_w[0], e_norm_cq[0],
                      e_norm_ckv[0], e_w_uq[0], e_w_uk[0], e_w_uv[0], e_w_qidx[0], e_w_out[0], e_norm_ffn[0],
                      e_ffn_gate[0], e_ffn_up[0], e_ffn_down[0])
    out = _odd_layer(x2d, bsz, seq, o_norm_mix[0], o_w_in[0], o_norm_v[0], o_w_s[0], o_b_s[0], o_w_out[0],
                     o_norm_ffn[0], o_router[0], o_exp_gate[0], o_exp_up[0], o_exp_down[0], final_norm)
    return out.reshape(bsz, seq, d)
```

```python
import functools
import math

import jax
import jax.numpy as jnp
from jax import lax
from jax.experimental import pallas as pl
from jax.experimental.pallas import tpu as pltpu

F32 = jnp.float32
BF16 = jnp.bfloat16

EPS = 1e-6
LANES = 128
COUNT_ROWS = 64
HEAD_DIM = 128
Q_RANK = 512
KV_RANK = 256
IDX_HEADS = 16
IDX_DIM = 64
IDX_SCALE = (IDX_DIM ** -0.5) * (IDX_HEADS ** -0.5)
TOPK_MAX = 256
DSA_KEY_CHUNK = 256
REL_BUCKETS = 32
REL_MAX_DIST = 128
CONV_WIDTH = 3
CHUNK = 128
TOP_K = 2
LOG2_E = math.log2(math.e)
INT_MIN = -(2 ** 31)
MIB = 1024 * 1024


def _params(sem, vmem_mib):
    return pltpu.CompilerParams(dimension_semantics=sem, vmem_limit_bytes=vmem_mib * MIB)


def _rms(x, g):
    return x * lax.rsqrt(jnp.mean(x * x, axis=-1, keepdims=True) + EPS) * g


def _dot(a, b):
    return jnp.dot(a, b, preferred_element_type=F32)


def _dot_nt(a, b):
    return lax.dot_general(a, b, (((1,), (1,)), ((), ())), preferred_element_type=F32)


def _norm_matmul_kernel(x_ref, g_ref, w_ref, o_ref, h_ref):
    @pl.when(pl.program_id(1) == 0)
    def _():
        h_ref[...] = _rms(x_ref[...], g_ref[...]).astype(BF16)

    o_ref[...] = _dot(h_ref[...], w_ref[...]).astype(o_ref.dtype)


def norm_matmul(x, g, w, tm=1024, tn=1024):
    n, d = x.shape
    nout = w.shape[1]
    return pl.pallas_call(
        _norm_matmul_kernel,
        grid=(n // tm, nout // tn),
        in_specs=[pl.BlockSpec((tm, d), lambda i, j: (i, 0)),
                  pl.BlockSpec((1, d), lambda i, j: (0, 0)),
                  pl.BlockSpec((d, tn), lambda i, j: (0, j))],
        out_specs=pl.BlockSpec((tm, tn), lambda i, j: (i, j)),
        out_shape=jax.ShapeDtypeStruct((n, nout), BF16),
        scratch_shapes=[pltpu.VMEM((tm, d), BF16)],
        compiler_params=_params(("parallel", "arbitrary"), 40),
        name="norm_matmul",
    )(x, g.reshape(1, d), w)


def _conv_kernel(b_ref, c_ref, x_ref, w_ref, o_ref):
    z = c_ref[0].astype(F32) * x_ref[0].astype(F32)
    row = lax.broadcasted_iota(jnp.int32, z.shape, 0)
    y = w_ref[CONV_WIDTH - 1:CONV_WIDTH, :] * z
    for lag in range(1, CONV_WIDTH):
        zl = jnp.where(row >= lag, pltpu.roll(z, lag, 0), 0.0)
        y = y + w_ref[CONV_WIDTH - 1 - lag:CONV_WIDTH - lag, :] * zl
    o_ref[0] = (b_ref[0].astype(F32) * y).astype(o_ref.dtype)


def conv_mixer(proj, conv_w, a_width, tc=256):
    bsz, seq, _ = proj.shape
    nb = a_width // tc
    return pl.pallas_call(
        _conv_kernel,
        grid=(bsz, nb),
        in_specs=[pl.BlockSpec((1, seq, tc), lambda b, c: (b, 0, c)),
                  pl.BlockSpec((1, seq, tc), lambda b, c: (b, 0, nb + c)),
                  pl.BlockSpec((1, seq, tc), lambda b, c: (b, 0, 2 * nb + c)),
                  pl.BlockSpec((CONV_WIDTH, tc), lambda b, c: (0, c))],
        out_specs=pl.BlockSpec((1, seq, tc), lambda b, c: (b, 0, c)),
        out_shape=jax.ShapeDtypeStruct((bsz, seq, a_width), BF16),
        compiler_params=_params(("parallel", "parallel"), 40),
        name="conv_mixer",
    )(proj, proj, proj, conv_w)


def _dsa_prep_kernel(cq_ref, ckv_ref, w_ref, gq_ref, gkv_ref, wuq_ref, wukt_ref, wqi_ref,
                     qlat_ref, qidx_ref, ckvo_ref, ckvt_ref, wt_ref):
    tm = cq_ref.shape[1]
    cqn = _rms(cq_ref[0].astype(F32), gq_ref[...]).astype(BF16)
    ckvn = _rms(ckv_ref[0].astype(F32), gkv_ref[...])
    ckvo_ref[0] = ckvn.astype(BF16)
    for c in range(tm // DSA_KEY_CHUNK):
        rows = slice(c * DSA_KEY_CHUNK, (c + 1) * DSA_KEY_CHUNK)
        ckvt_ref[0, c] = jnp.transpose(ckvn[rows, :]).astype(BF16)
    wt_ref[0] = jnp.transpose(w_ref[0].astype(F32))[:IDX_HEADS, :] * IDX_SCALE
    q = _dot(cqn, wuq_ref[...]).astype(BF16)
    scale = HEAD_DIM ** -0.5 * LOG2_E
    for h in range(wukt_ref.shape[0]):
        qh = q[:, h * HEAD_DIM:(h + 1) * HEAD_DIM]
        qlat_ref[0, h] = (_dot(qh, wukt_ref[h]) * scale).astype(BF16)
    qi = _dot(cqn, wqi_ref[...]).astype(BF16)
    for g in range(IDX_HEADS):
        qidx_ref[0, g] = qi[:, g * LANES:(g + 1) * LANES]


def dsa_prep(proj, g_cq, g_ckv, w_uq, w_ukt, w_qidx_pad, cq_col, ckv_col, widx_col, tm=256):
    bsz, seq, _ = proj.shape
    nh = w_ukt.shape[0]
    full = lambda *shape: pl.BlockSpec(shape, lambda b, i: (0,) * len(shape))
    return pl.pallas_call(
        _dsa_prep_kernel,
        grid=(bsz, seq // tm),
        in_specs=[pl.BlockSpec((1, tm, Q_RANK), lambda b, i: (b, i, cq_col // Q_RANK)),
                  pl.BlockSpec((1, tm, KV_RANK), lambda b, i: (b, i, ckv_col // KV_RANK)),
                  pl.BlockSpec((1, tm, LANES), lambda b, i: (b, i, widx_col // LANES)),
                  full(1, Q_RANK), full(1, KV_RANK),
                  full(Q_RANK, nh * HEAD_DIM), full(nh, HEAD_DIM, KV_RANK),
                  full(Q_RANK, IDX_HEADS * LANES)],
        out_specs=[pl.BlockSpec((1, nh, tm, KV_RANK), lambda b, i: (b, 0, i, 0)),
                   pl.BlockSpec((1, IDX_HEADS, tm, LANES), lambda b, i: (b, 0, i, 0)),
                   pl.BlockSpec((1, tm, KV_RANK), lambda b, i: (b, i, 0)),
                   pl.BlockSpec((1, tm // DSA_KEY_CHUNK, KV_RANK, DSA_KEY_CHUNK), lambda b, i: (b, i, 0, 0)),
                   pl.BlockSpec((1, IDX_HEADS, tm), lambda b, i: (b, 0, i))],
        out_shape=[jax.ShapeDtypeStruct((bsz, nh, seq, KV_RANK), BF16),
                   jax.ShapeDtypeStruct((bsz, IDX_HEADS, seq, LANES), BF16),
                   jax.ShapeDtypeStruct((bsz, seq, KV_RANK), BF16),
                   jax.ShapeDtypeStruct((bsz, seq // DSA_KEY_CHUNK, KV_RANK, DSA_KEY_CHUNK), BF16),
                   jax.ShapeDtypeStruct((bsz, IDX_HEADS, seq), F32)],
        compiler_params=_params(("parallel", "parallel"), 40),
        name="dsa_prep",
    )(proj, proj, proj, g_cq.reshape(1, -1), g_ckv.reshape(1, -1), w_uq, w_ukt, w_qidx_pad)


def _dsa_kernel(qlat_ref, qidx_ref, wt_ref, kidx_ref, ckv_ref, ckvt_ref, toep_ref, wuvt_ref, o_ref,
                sc_ref, key_ref, p_ref, m_ref, l_ref, alpha_ref, ties_seen_ref, acc_ref, *, topk):
    tq = LANES
    tk = DSA_KEY_CHUNK
    nh = qlat_ref.shape[1]
    i = pl.program_id(1)
    n_chunks = (i * tq) // tk + 1
    kpos = lax.broadcasted_iota(jnp.int32, (tk, tq), 0)
    qpos = lax.broadcasted_iota(jnp.int32, (tk, tq), 1)
    wt = wt_ref[0]
    qi = qidx_ref[0].reshape(IDX_HEADS * tq, LANES)

    def score_body(c, carry):
        kj = kidx_ref[0, pl.ds(pl.multiple_of(c * tk, tk), tk), :]
        sg = _dot_nt(kj, qi)
        s = jnp.zeros((tk, tq), F32)
        for g in range(IDX_HEADS):
            s = s + wt[g:g + 1, :] * jnp.maximum(sg[:, g * tq:(g + 1) * tq], 0.0)
        sc_ref[c] = s
        bits = pltpu.bitcast(s, jnp.int32)
        key = bits ^ ((bits >> 31) & 0x7FFFFFFF)
        causal = (c * tk + kpos) <= (i * tq + qpos)
        key_ref[c] = jnp.where(causal, key, INT_MIN)
        return carry

    lax.fori_loop(0, n_chunks, score_body, 0)

    kvec = jnp.minimum(topk, i * tq + qpos[:1, :] + 1).astype(F32)

    def count_ge(cand):
        def body(c, acc):
            hit = jnp.where(key_ref[c] >= cand, 1.0, 0.0)
            return acc + jnp.sum(hit.reshape(tk // COUNT_ROWS, COUNT_ROWS, tq), axis=0)
        acc = lax.fori_loop(0, n_chunks, body, jnp.zeros((COUNT_ROWS, tq), F32))
        return jnp.sum(acc, axis=0, keepdims=True)

    thr = jnp.where(count_ge(jnp.zeros((1, tq), jnp.int32)) >= kvec, 0, INT_MIN).astype(jnp.int32)

    def bit_body(b, thr):
        cand = thr | jnp.left_shift(jnp.int32(1), 30 - b)
        return jnp.where(count_ge(cand) >= kvec, cand, thr)

    thr = lax.fori_loop(0, 31, bit_body, thr)
    ties_wanted = kvec - count_ge(thr + 1)
    ties_seen_ref[...] = jnp.zeros(ties_seen_ref.shape, F32)
    kr = lax.broadcasted_iota(jnp.int32, (tk, tk), 0)
    kc = lax.broadcasted_iota(jnp.int32, (tk, tk), 1)
    upto = jnp.where(kc <= kr, 1.0, 0.0).astype(BF16)

    m_ref[...] = jnp.full(m_ref.shape, -jnp.inf, F32)
    l_ref[...] = jnp.zeros(l_ref.shape, F32)
    acc_ref[...] = jnp.zeros(acc_ref.shape, F32)
    q = qlat_ref[0].reshape(nh * tq, KV_RANK)

    def att_body(c, carry):
        cj = ckv_ref[0, pl.ds(pl.multiple_of(c * tk, tk), tk), :]
        key = key_ref[c]
        tie = key == thr
        tie_rank = _dot(upto, jnp.where(tie, 1.0, 0.0).astype(BF16)) + ties_seen_ref[...]
        ties_seen_ref[...] = tie_rank[tk - 1:tk, :]
        sel = (key > thr) | (tie & (tie_rank <= ties_wanted))
        base = jnp.where(sel, sc_ref[c] * LOG2_E, -jnp.inf)
        mdist = [jnp.clip(i - (c * (tk // tq) + r), 0, 2) for r in range(tk // tq)]
        lg = _dot_nt(cj, q)
        for h in range(nh):
            cols = slice(h * tq, (h + 1) * tq)
            bias = jnp.concatenate([toep_ref[h, m] for m in mdist], axis=0)
            lh = lg[:, cols] + bias + base
            m_old = m_ref[:, cols]
            m_new = jnp.maximum(m_old, jnp.max(lh, axis=0, keepdims=True))
            m_safe = jnp.where(m_new == -jnp.inf, 0.0, m_new)
            alpha = jnp.exp2(m_old - m_safe)
            p = jnp.exp2(lh - m_safe)
            m_ref[:, cols] = m_new
            l_ref[:, cols] = alpha * l_ref[:, cols] + jnp.sum(p, axis=0, keepdims=True)
            alpha_ref[:, cols] = alpha
            p_ref[:, cols] = p.astype(BF16)
        acc_ref[...] = alpha_ref[...] * acc_ref[...] + _dot(ckvt_ref[0, c], p_ref[...])
        return carry

    lax.fori_loop(0, n_chunks, att_body, 0)

    for h in range(nh):
        cols = slice(h * tq, (h + 1) * tq)
        o_lat_t = (acc_ref[:, cols] / l_ref[:, cols]).astype(BF16)
        y_t = _dot(wuvt_ref[h], o_lat_t)
        o_ref[0, :, h * HEAD_DIM:(h + 1) * HEAD_DIM] = jnp.transpose(y_t).astype(o_ref.dtype)


def dsa_attention(qlat, qidx, wt, proj, ckv, ckvt, toep, w_uvt, kidx_col, topk):
    bsz, nh, seq, _ = qlat.shape
    tq = LANES
    tk = DSA_KEY_CHUNK
    nkb = seq // tk
    full = lambda *shape: pl.BlockSpec(shape, lambda b, i: (0,) * len(shape))
    return pl.pallas_call(
        functools.partial(_dsa_kernel, topk=topk),
        grid=(bsz, seq // tq),
        in_specs=[pl.BlockSpec((1, nh, tq, KV_RANK), lambda b, i: (b, 0, i, 0)),
                  pl.BlockSpec((1, IDX_HEADS, tq, LANES), lambda b, i: (b, 0, i, 0)),
                  pl.BlockSpec((1, IDX_HEADS, tq), lambda b, i: (b, 0, i)),
                  pl.BlockSpec((1, seq, LANES), lambda b, i: (b, 0, kidx_col // LANES)),
                  pl.BlockSpec((1, seq, KV_RANK), lambda b, i: (b, 0, 0)),
                  pl.BlockSpec((1, nkb, KV_RANK, tk), lambda b, i: (b, 0, 0, 0)),
                  full(nh, 3, tq, tq),
                  full(nh, HEAD_DIM, KV_RANK)],
        out_specs=pl.BlockSpec((1, tq, nh * HEAD_DIM), lambda b, i: (b, i, 0)),
        out_shape=jax.ShapeDtypeStruct((bsz, seq, nh * HEAD_DIM), BF16),
        scratch_shapes=[pltpu.VMEM((nkb, tk, tq), F32),
                        pltpu.VMEM((nkb, tk, tq), jnp.int32),
                        pltpu.VMEM((tk, nh * tq), BF16),
                        pltpu.VMEM((1, nh * tq), F32),
                        pltpu.VMEM((1, nh * tq), F32),
                        pltpu.VMEM((1, nh * tq), F32),
                        pltpu.VMEM((1, tq), F32),
                        pltpu.VMEM((KV_RANK, nh * tq), F32)],
        compiler_params=_params(("parallel", "arbitrary"), 40),
        name="dsa_attention",
    )(qlat, qidx, wt, proj, ckv, ckvt, toep, w_uvt)


def _rel_bucket(dist):
    n = jnp.maximum(dist, 0)
    max_exact = REL_BUCKETS // 2
    nf = jnp.maximum(n, max_exact).astype(F32)
    large = max_exact + (jnp.log(nf / max_exact) / math.log(REL_MAX_DIST / max_exact)
                         * (REL_BUCKETS - max_exact)).astype(jnp.int32)
    large = jnp.minimum(large, REL_BUCKETS - 1)
    return jnp.where(n < max_exact, n, large)


def _toeplitz_bias(rel_bias):
    t = LANES
    r = jnp.arange(t, dtype=jnp.int32)
    tiles = []
    for m in range(3):
        dist = m * t + r[None, :] - r[:, None] if m < 2 else jnp.full((t, t), 2 * REL_MAX_DIST, jnp.int32)
        tiles.append(_rel_bucket(dist))
    bucket = jnp.stack(tiles)
    bias = jnp.zeros((rel_bias.shape[1],) + bucket.shape, F32)
    for k in range(REL_BUCKETS):
        bias = jnp.where(bucket[None] == k, rel_bias[k][:, None, None, None], bias)
    return bias * LOG2_E


def _out_proj_kernel(ya_ref, yb_ref, wa_ref, wb_ref, x_ref, o_ref):
    o_ref[...] = x_ref[...] + _dot(ya_ref[...], wa_ref[...]) + _dot(yb_ref[...], wb_ref[...])


def out_proj(ya, yb, w_out, x, tm=1024, tn=1024):
    n, ka = ya.shape
    kb = yb.shape[1]
    d = x.shape[1]
    wa, wb = w_out[:ka], w_out[ka:]
    return pl.pallas_call(
        _out_proj_kernel,
        grid=(n // tm, d // tn),
        in_specs=[pl.BlockSpec((tm, ka), lambda i, j: (i, 0)),
                  pl.BlockSpec((tm, kb), lambda i, j: (i, 0)),
                  pl.BlockSpec((ka, tn), lambda i, j: (0, j)),
                  pl.BlockSpec((kb, tn), lambda i, j: (0, j)),
                  pl.BlockSpec((tm, tn), lambda i, j: (i, j))],
        out_specs=pl.BlockSpec((tm, tn), lambda i, j: (i, j)),
        out_shape=jax.ShapeDtypeStruct((n, d), F32),
        compiler_params=_params(("parallel", "parallel"), 40),
        name="out_proj",
    )(ya, yb, wa, wb, x)


def _silu(a):
    return a / (1.0 + jnp.exp(-a))


def _ffn_kernel(x_ref, g_ref, wg_ref, wu_ref, wd_ref, o_ref, h_ref):
    @pl.when(pl.program_id(1) == 0)
    def _():
        x = x_ref[...]
        h_ref[...] = _rms(x, g_ref[...]).astype(BF16)
        o_ref[...] = x

    h = h_ref[...]
    act = (_silu(_dot(h, wg_ref[...])) * _dot(h, wu_ref[...])).astype(BF16)
    o_ref[...] += _dot(act, wd_ref[...])


def ffn(x, g, w_gate, w_up, w_down, tm=1024, tf=512):
    n, d = x.shape
    dff = w_gate.shape[1]
    return pl.pallas_call(
        _ffn_kernel,
        grid=(n // tm, dff // tf),
        in_specs=[pl.BlockSpec((tm, d), lambda i, f: (i, 0)),
                  pl.BlockSpec((1, d), lambda i, f: (0, 0)),
                  pl.BlockSpec((d, tf), lambda i, f: (0, f)),
                  pl.BlockSpec((d, tf), lambda i, f: (0, f)),
                  pl.BlockSpec((tf, d), lambda i, f: (f, 0))],
        out_specs=pl.BlockSpec((tm, d), lambda i, f: (i, 0)),
        out_shape=jax.ShapeDtypeStruct((n, d), F32),
        scratch_shapes=[pltpu.VMEM((tm, d), BF16)],
        compiler_params=_params(("parallel", "arbitrary"), 58),
        name="ffn",
    )(x, g.reshape(1, d), w_gate, w_up, w_down)


def _gelu(x):
    return x * (0.5 * (1.0 + jnp.tanh(math.sqrt(2.0 / math.pi) * (x + 0.044715 * (x * x * x)))))


def _gmlp_kernel(cu_ref, cv_ref, gv_ref, ws_ref, bst_ref, o_ref):
    tm = cu_ref.shape[0]
    ng = ws_ref.shape[0]
    gw = cu_ref.shape[1] // ng
    u = _gelu(cu_ref[...].astype(F32))
    vn = _rms(_gelu(cv_ref[...].astype(F32)), gv_ref[...]).astype(BF16)
    row = lax.broadcasted_iota(jnp.int32, (CHUNK, CHUNK), 0)
    col = lax.broadcasted_iota(jnp.int32, (CHUNK, CHUNK), 1)
    for g in range(ng):
        wsg = jnp.where(row >= col, ws_ref[g], 0.0).astype(BF16)
        bias = bst_ref[:, g:g + 1]
        cols = slice(g * gw, (g + 1) * gw)
        for c in range(tm // CHUNK):
            rows = slice(c * CHUNK, (c + 1) * CHUNK)
            mixed = _dot(wsg, vn[rows, cols]) + bias
            o_ref[rows, cols] = (u[rows, cols] * mixed).astype(o_ref.dtype)


def gmlp_mixer(proj, g_v, w_s, b_s_t, c_width, tm=256):
    n = proj.shape[0]
    ng = w_s.shape[0]
    return pl.pallas_call(
        _gmlp_kernel,
        grid=(n // tm,),
        in_specs=[pl.BlockSpec((tm, c_width), lambda i: (i, 0)),
                  pl.BlockSpec((tm, c_width), lambda i: (i, 1)),
                  pl.BlockSpec((1, c_width), lambda i: (0, 0)),
                  pl.BlockSpec((ng, CHUNK, CHUNK), lambda i: (0, 0, 0)),
                  pl.BlockSpec((CHUNK, ng), lambda i: (0, 0))],
        out_specs=pl.BlockSpec((tm, c_width), lambda i: (i, 0)),
        out_shape=jax.ShapeDtypeStruct((n, c_width), BF16),
        compiler_params=_params(("parallel",), 40),
        name="gmlp_mixer",
    )(proj, proj, g_v.reshape(1, -1), w_s, b_s_t)


def _sb_kernel(q_ref, k_ref, v_ref, u_ref, o_ref, acc_ref, carry_ref, *, tk):
    tq = q_ref.shape[1]
    i = pl.program_id(2)
    acc_ref[...] = jnp.zeros(acc_ref.shape, F32)
    carry_ref[...] = jnp.zeros(carry_ref.shape, F32)
    q = q_ref[0]
    tpos = i * tq + lax.broadcasted_iota(jnp.int32, (tq, tk), 0)
    col = lax.broadcasted_iota(jnp.int32, (tq, tk), 1)
    n_diag = tq // tk

    def block(j, diagonal):
        start = pl.multiple_of(j * tk, tk)
        for h in range(q_ref.shape[2] // HEAD_DIM):
            cols = slice(h * HEAD_DIM, (h + 1) * HEAD_DIM)
            kj = k_ref[0, pl.ds(start, tk), cols]
            vj = v_ref[0, pl.ds(start, tk), cols]
            z = _dot_nt(q[:, cols], kj)
            log_keep = -(jnp.maximum(z, 0.0) + jnp.log2(1.0 + jnp.exp2(-jnp.abs(z))))
            log_beta = z + log_keep
            if diagonal:
                past = (j * tk + col) < tpos
                log_keep = jnp.where(past, log_keep, 0.0)
            hi = log_keep.astype(BF16)
            lo = (log_keep - hi.astype(F32)).astype(BF16)
            excl = _dot(hi, u_ref[...]) + _dot(lo, u_ref[...])
            wts = jnp.exp2(log_beta + excl + carry_ref[h])
            if diagonal:
                wts = jnp.where(past, wts, 0.0)
            acc_ref[:, cols] += _dot(wts.astype(BF16), vj)
            carry_ref[h] += jnp.sum(log_keep, axis=1, keepdims=True)

    for d in range(n_diag):
        block((i + 1) * n_diag - 1 - d, True)

    def body(step, c):
        block(i * n_diag - 1 - step, False)
        return c

    lax.fori_loop(0, i * n_diag, body, 0)
    o_ref[0] = acc_ref[...].astype(o_ref.dtype)


def stick_breaking(proj, nh, q_col, k_col, v_col, tq=512, tk=256, heads_per_step=2):
    bsz, seq, _ = proj.shape
    tq = min(tq, seq)
    tk = min(tk, tq)
    r = jnp.arange(tk, dtype=jnp.int32)
    upper = (r[:, None] > r[None, :]).astype(BF16)
    hw = heads_per_step * HEAD_DIM
    qb, kb, vb = q_col // hw, k_col // hw, v_col // hw
    assert q_col % hw == 0 and k_col % hw == 0 and v_col % hw == 0 and nh % heads_per_step == 0
    return pl.pallas_call(
        functools.partial(_sb_kernel, tk=tk),
        grid=(bsz, nh // heads_per_step, seq // tq),
        in_specs=[pl.BlockSpec((1, tq, hw), lambda b, h, i: (b, i, qb + h)),
                  pl.BlockSpec((1, seq, hw), lambda b, h, i: (b, 0, kb + h)),
                  pl.BlockSpec((1, seq, hw), lambda b, h, i: (b, 0, vb + h)),
                  pl.BlockSpec((tk, tk), lambda b, h, i: (0, 0))],
        out_specs=pl.BlockSpec((1, tq, hw), lambda b, h, i: (b, i, h)),
        out_shape=jax.ShapeDtypeStruct((bsz, seq, nh * HEAD_DIM), BF16),
        scratch_shapes=[pltpu.VMEM((tq, hw), F32), pltpu.VMEM((heads_per_step, tq, 1), F32)],
        compiler_params=_params(("parallel", "parallel", "arbitrary"), 40),
        name="stick_breaking",
    )(proj, proj, proj, upper)


MOE_CHUNK = 64
MOE_SEG_ALIGN = 16
MOE_ROW_BLOCK = 1024
MOE_SUB_BLOCK = 256


def _moe_route_kernel(x_ref, g_ref, rhi_ref, rlo_ref, h_ref, comb_ref, cnt_ref, *, n_exp):
    tm = x_ref.shape[0]
    h32 = _rms(x_ref[...], g_ref[...])
    hi = h32.astype(BF16)
    lo = (h32 - hi.astype(F32)).astype(BF16)
    h_ref[...] = hi
    logits = _dot(hi, rhi_ref[...]) + (_dot(hi, rlo_ref[...]) + _dot(lo, rhi_ref[...]))
    lane = lax.broadcasted_iota(jnp.int32, (tm, LANES), 1)
    lanef = lane.astype(F32)
    lg = jnp.where(lane < n_exp, logits, -jnp.inf)
    m1 = jnp.max(lg, axis=1, keepdims=True)
    i1 = jnp.min(jnp.where(lg == m1, lanef, float(LANES)), axis=1, keepdims=True)
    lg2 = jnp.where(lanef == i1, -jnp.inf, lg)
    m2 = jnp.max(lg2, axis=1, keepdims=True)
    i2 = jnp.min(jnp.where(lg2 == m2, lanef, float(LANES)), axis=1, keepdims=True)
    e2 = jnp.exp(m2 - m1)
    g1 = 1.0 / (1.0 + e2)
    comb = jnp.where(lanef == i1, g1, 0.0) + jnp.where(lanef == i2, e2 * g1, 0.0)
    comb_ref[...] = comb
    cnt_ref[0] = jnp.sum(jnp.where(comb > 0.0, 1.0, 0.0), axis=0, keepdims=True)


def moe_route(x, g, r_hi, r_lo, n_exp, tm):
    n, d = x.shape
    nt = n // tm
    return pl.pallas_call(
        functools.partial(_moe_route_kernel, n_exp=n_exp),
        grid=(nt,),
        in_specs=[pl.BlockSpec((tm, d), lambda i: (i, 0)),
                  pl.BlockSpec((1, d), lambda i: (0, 0)),
                  pl.BlockSpec((d, LANES), lambda i: (0, 0)),
                  pl.BlockSpec((d, LANES), lambda i: (0, 0))],
        out_specs=[pl.BlockSpec((tm, d), lambda i: (i, 0)),
                   pl.BlockSpec((tm, LANES), lambda i: (i, 0)),
                   pl.BlockSpec((1, 1, LANES), lambda i: (i, 0, 0))],
        out_shape=[jax.ShapeDtypeStruct((n, d), BF16),
                   jax.ShapeDtypeStruct((n, LANES), F32),
                   jax.ShapeDtypeStruct((nt, 1, LANES), F32)],
        compiler_params=_params(("parallel",), 40),
        name="moe_route",
    )(x, g.reshape(1, d), r_hi, r_lo)


def _moe_plan(cnt, n_rows_static, max_pieces):
    nt, n_exp = cnt.shape
    i32 = jnp.int32
    cp = (cnt + MOE_SEG_ALIGN - 1) // MOE_SEG_ALIGN * MOE_SEG_ALIGN
    reg = (jnp.sum(cp, axis=0) + MOE_CHUNK + MOE_ROW_BLOCK - 1) // MOE_ROW_BLOCK * MOE_ROW_BLOCK
    base = jnp.cumsum(reg) - reg
    seg = base[None, :] + jnp.cumsum(cp, axis=0) - cp
    nce = (cnt + MOE_CHUNK - 1) // MOE_CHUNK
    cum = jnp.cumsum(nce, axis=1)
    n_pieces = cum[:, -1]
    k = jnp.arange(max_pieces, dtype=i32)
    ek = jnp.minimum(jnp.sum(cum[:, None, :] <= k[None, :, None], axis=-1), n_exp - 1).astype(i32)
    is_e = ek[:, :, None] == jnp.arange(n_exp, dtype=i32)
    pick = lambda tbl: jnp.sum(jnp.where(is_e, tbl[:, None, :], 0), axis=-1)
    rk = k[None, :] - pick(cum - nce)
    off = pick(seg) + rk * MOE_CHUNK
    valid = k[None, :] < n_pieces[:, None]
    rk = jnp.where(valid, rk, 0).astype(i32)
    off = jnp.where(valid, off, 0).astype(i32)
    n_blocks = n_rows_static // MOE_ROW_BLOCK
    cb = jnp.cumsum(reg // MOE_ROW_BLOCK)
    blk = jnp.arange(n_blocks, dtype=i32)
    blk_exp = jnp.minimum(jnp.sum(cb[None, :] <= blk[:, None], axis=1), n_exp - 1)
    is_be = blk_exp[:, None] == jnp.arange(n_exp, dtype=i32)
    pick_b = lambda vec: jnp.sum(jnp.where(is_be, vec[None, :], 0), axis=-1)
    used = jnp.sum(cp, axis=0)
    blk_rows = jnp.clip(pick_b(used) - (blk * MOE_ROW_BLOCK - pick_b(base)), 0, MOE_ROW_BLOCK)
    return (ek.reshape(-1), rk.reshape(-1), off.reshape(-1), n_pieces.astype(i32),
            blk_exp.astype(i32), blk_rows.astype(i32), cb[-1:].astype(i32))


def _moe_compact_kernel(ek_ref, rk_ref, off_ref, np_ref, h_ref, comb_ref, a_in, g_in, a_out, g_out,
                        pos_ref, p_ref, stage_ref, gst_ref, sem, *, max_pieces):
    del a_in, g_in
    i = pl.program_id(0)
    tm = h_ref.shape[0]
    n_pieces = np_ref[i]
    comb = comb_ref[...]
    mask_t = jnp.transpose(jnp.where(comb > 0.0, 1.0, 0.0))
    r0 = lax.broadcasted_iota(jnp.int32, (tm, tm), 0)
    r1 = lax.broadcasted_iota(jnp.int32, (tm, tm), 1)
    before = jnp.where(r0 < r1, 1.0, 0.0).astype(BF16)
    pos_t = _dot(mask_t.astype(BF16), before)
    pos_ref[...] = jnp.where(mask_t > 0.0, pos_t, -1.0)
    slot = lax.broadcasted_iota(jnp.int32, (MOE_CHUNK, tm), 0).astype(F32)
    for k in range(max_pieces):
        rows = slice(k * MOE_CHUNK, (k + 1) * MOE_CHUNK)
        e_k = ek_ref[i * max_pieces + k]
        r_k = rk_ref[i * max_pieces + k]

        @pl.when(k < n_pieces)
        def _():
            want = slot + (r_k * MOE_CHUNK).astype(F32)
            p_ref[rows] = jnp.where(pos_ref[pl.ds(e_k, 1), :] == want, 1.0, 0.0).astype(BF16)

        @pl.when(k >= n_pieces)
        def _():
            p_ref[rows] = jnp.zeros((MOE_CHUNK, tm), BF16)

    d = h_ref.shape[1]
    nc = 512
    for c in range(d // nc):
        stage_ref[:, c * nc:(c + 1) * nc] = _dot(p_ref[...], h_ref[:, c * nc:(c + 1) * nc]).astype(BF16)
    c_hi = comb.astype(BF16)
    c_mid = (comb - c_hi.astype(F32)).astype(BF16)
    c_lo = (comb - c_hi.astype(F32) - c_mid.astype(F32)).astype(BF16)
    p = p_ref[...]
    gst_ref[...] = _dot(p, c_hi) + _dot(p, c_mid) + _dot(p, c_lo)

    def copies(k):
        rows = pl.ds(k * MOE_CHUNK, MOE_CHUNK)
        dst = pl.ds(pl.multiple_of(off_ref[i * max_pieces + k], MOE_SEG_ALIGN), MOE_CHUNK)
        return (pltpu.make_async_copy(stage_ref.at[rows], a_out.at[dst], sem.at[0]),
                pltpu.make_async_copy(gst_ref.at[rows], g_out.at[dst], sem.at[1]))

    for k in range(max_pieces):
        @pl.when(k < n_pieces)
        def _():
            for cp in copies(k):
                cp.start()
    for k in range(max_pieces):
        @pl.when(k < n_pieces)
        def _():
            for cp in copies(k):
                cp.wait()


def moe_compact(plan, h, comb, n_rows, tm, max_pieces):
    ek, rk, off, n_pieces = plan
    n, d = h.shape
    a0 = jnp.zeros((n_rows, d), BF16)
    g0 = jnp.zeros((n_rows, LANES), F32)
    any_spec = pl.BlockSpec(memory_space=pl.ANY)
    return pl.pallas_call(
        functools.partial(_moe_compact_kernel, max_pieces=max_pieces),
        grid_spec=pltpu.PrefetchScalarGridSpec(
            num_scalar_prefetch=4,
            grid=(n // tm,),
            in_specs=[pl.BlockSpec((tm, d), lambda i, *_: (i, 0)),
                      pl.BlockSpec((tm, LANES), lambda i, *_: (i, 0)),
                      any_spec, any_spec],
            out_specs=[any_spec, any_spec],
            scratch_shapes=[pltpu.VMEM((LANES, tm), F32),
                            pltpu.VMEM((max_pieces * MOE_CHUNK, tm), BF16),
                            pltpu.VMEM((max_pieces * MOE_CHUNK, d), BF16),
                            pltpu.VMEM((max_pieces * MOE_CHUNK, LANES), F32),
                            pltpu.SemaphoreType.DMA((2,))]),
        out_shape=[jax.ShapeDtypeStruct((n_rows, d), BF16), jax.ShapeDtypeStruct((n_rows, LANES), F32)],
        input_output_aliases={6: 0, 7: 1},
        compiler_params=_params(("arbitrary",), 48),
        name="moe_compact",
    )(ek, rk, off, n_pieces, h, comb, a0, g0)


def _moe_expert_kernel(be_ref, br_ref, nv_ref, a_ref, gs_ref, wg_ref, wu_ref, wd_ref, y_ref, acc_ref):
    b = pl.program_id(0)
    f = pl.program_id(1)
    rb = a_ref.shape[0]
    n_rows = br_ref[b]
    expert = be_ref[b]
    busy = b < nv_ref[0]

    def step(rows):
        @pl.when(f == 0)
        def _():
            acc_ref[rows] = jnp.zeros((rows.stop - rows.start, acc_ref.shape[1]), F32)

        a = a_ref[rows]
        act = (_silu(_dot(a, wg_ref[0].astype(BF16))) * _dot(a, wu_ref[0].astype(BF16))).astype(BF16)
        acc_ref[rows] += _dot(act, wd_ref[0].astype(BF16))

        @pl.when(f == pl.num_programs(1) - 1)
        def _():
            gs = gs_ref[rows]
            lane = lax.broadcasted_iota(jnp.int32, gs.shape, 1)
            gate = jnp.sum(jnp.where(lane == expert, gs, 0.0), axis=1, keepdims=True)
            y_ref[rows] = (acc_ref[rows] * gate).astype(y_ref.dtype)

    @pl.when(busy & (n_rows == rb))
    def _():
        step(slice(0, rb))

    for sb in range(rb // MOE_SUB_BLOCK):
        rows = slice(sb * MOE_SUB_BLOCK, (sb + 1) * MOE_SUB_BLOCK)

        @pl.when(busy & (n_rows < rb) & (sb * MOE_SUB_BLOCK < n_rows))
        def _():
            step(rows)

        @pl.when((jnp.logical_not(busy) | (sb * MOE_SUB_BLOCK >= n_rows)) & (f == 0))
        def _():
            y_ref[rows] = jnp.zeros((MOE_SUB_BLOCK, y_ref.shape[1]), y_ref.dtype)


def moe_experts(blk_exp, blk_rows, n_valid, a_sorted, g_sorted, w_gate, w_up, w_down, tf=512):
    n_rows, d = a_sorted.shape
    n_exp, _, dff = w_gate.shape
    nf = dff // tf
    rb = MOE_ROW_BLOCK

    def blk(b, nv):
        return jnp.minimum(b, nv[0] - 1)

    def fblk(b, f, nv):
        return jnp.where(b < nv[0], f, nf - 1)

    return pl.pallas_call(
        _moe_expert_kernel,
        grid_spec=pltpu.PrefetchScalarGridSpec(
            num_scalar_prefetch=3,
            grid=(n_rows // rb, nf),
            in_specs=[pl.BlockSpec((rb, d), lambda b, f, be, br, nv: (blk(b, nv), 0)),
                      pl.BlockSpec((rb, LANES), lambda b, f, be, br, nv: (blk(b, nv), 0)),
                      pl.BlockSpec((1, d, tf), lambda b, f, be, br, nv: (be[blk(b, nv)], 0, fblk(b, f, nv))),
                      pl.BlockSpec((1, d, tf), lambda b, f, be, br, nv: (be[blk(b, nv)], 0, fblk(b, f, nv))),
                      pl.BlockSpec((1, tf, d), lambda b, f, be, br, nv: (be[blk(b, nv)], fblk(b, f, nv), 0))],
            out_specs=pl.BlockSpec((rb, d), lambda b, f, be, br, nv: (b, 0)),
            scratch_shapes=[pltpu.VMEM((rb, d), F32)]),
        out_shape=jax.ShapeDtypeStruct((n_rows, d), BF16),
        compiler_params=_params(("arbitrary", "arbitrary"), 58),
        name="moe_experts",
    )(blk_exp, blk_rows, n_valid, a_sorted, g_sorted, w_gate, w_up, w_down)


def _moe_combine_kernel(ek_ref, rk_ref, off_ref, np_ref, x_ref, comb_ref, gf_ref, y_hbm, o_ref,
                        posb_ref, sel_ref, ybuf_ref, sem, *, max_pieces, n_exp):
    i = pl.program_id(0)
    tm = x_ref.shape[0]
    n_pieces = np_ref[i]

    def copy(k):
        src = pl.ds(pl.multiple_of(off_ref[i * max_pieces + k], MOE_SEG_ALIGN), MOE_CHUNK)
        return pltpu.make_async_copy(y_hbm.at[src], ybuf_ref.at[pl.ds(k * MOE_CHUNK, MOE_CHUNK)], sem.at[0])

    for k in range(max_pieces):
        @pl.when(k < n_pieces)
        def _():
            copy(k).start()

        @pl.when(k >= n_pieces)
        def _():
            ybuf_ref[k * MOE_CHUNK:(k + 1) * MOE_CHUNK, :] = jnp.zeros((MOE_CHUNK, ybuf_ref.shape[1]), BF16)

    mask = jnp.where(comb_ref[...] > 0.0, 1.0, 0.0)
    r0 = lax.broadcasted_iota(jnp.int32, (tm, tm), 0)
    r1 = lax.broadcasted_iota(jnp.int32, (tm, tm), 1)
    before = jnp.where(r1 < r0, 1.0, 0.0).astype(BF16)
    pos = jnp.where(mask > 0.0, _dot(before, mask.astype(BF16)), -1.0)
    for e in range(n_exp):
        posb_ref[e] = jnp.broadcast_to(pos[:, e:e + 1], (tm, LANES))
    per_tile = LANES // MOE_CHUNK
    lane = lax.broadcasted_iota(jnp.int32, (tm, LANES), 1)
    for kk in range(max_pieces // per_tile):
        hit = None
        for sub in range(per_tile):
            k = kk * per_tile + sub
            e_k = ek_ref[i * max_pieces + k]
            first = jnp.where(k < n_pieces, rk_ref[i * max_pieces + k] * MOE_CHUNK - sub * MOE_CHUNK, -2 - LANES)
            here = (lane >= sub * MOE_CHUNK) & (lane < (sub + 1) * MOE_CHUNK)
            match = here & (posb_ref[e_k] == (lane + first).astype(F32))
            hit = match if hit is None else (hit | match)
        sel_ref[:, kk * LANES:(kk + 1) * LANES] = jnp.where(hit, 1.0, 0.0).astype(BF16)

    for k in range(max_pieces):
        @pl.when(k < n_pieces)
        def _():
            copy(k).wait()

    y = x_ref[...] + _dot(sel_ref[...], ybuf_ref[...])
    o_ref[...] = _rms(y, gf_ref[...])


def moe_combine(plan, x, comb, g_final, y_sorted, n_exp, tm, max_pieces):
    ek, rk, off, n_pieces = plan
    n, d = x.shape
    return pl.pallas_call(
        functools.partial(_moe_combine_kernel, max_pieces=max_pieces, n_exp=n_exp),
        grid_spec=pltpu.PrefetchScalarGridSpec(
            num_scalar_prefetch=4,
            grid=(n // tm,),
            in_specs=[pl.BlockSpec((tm, d), lambda i, *_: (i, 0)),
                      pl.BlockSpec((tm, LANES), lambda i, *_: (i, 0)),
                      pl.BlockSpec((1, d), lambda i, *_: (0, 0)),
                      pl.BlockSpec(memory_space=pl.ANY)],
            out_specs=pl.BlockSpec((tm, d), lambda i, *_: (i, 0)),
            scratch_shapes=[pltpu.VMEM((n_exp, tm, LANES), F32),
                            pltpu.VMEM((tm, max_pieces * MOE_CHUNK), BF16),
                            pltpu.VMEM((max_pieces * MOE_CHUNK, d), BF16),
                            pltpu.SemaphoreType.DMA((1,))]),
        out_shape=jax.ShapeDtypeStruct((n, d), F32),
        compiler_params=_params(("arbitrary",), 56),
        name="moe_combine",
    )(ek, rk, off, n_pieces, x, comb, g_final.reshape(1, d), y_sorted)


def moe(x, g, r_hi, r_lo, w_gate, w_up, w_down, g_final, tm=512):
    n, d = x.shape
    n_exp = w_gate.shape[0]
    tm = min(tm, n // 2)
    nt = n // tm
    max_pieces = TOP_K * tm // MOE_CHUNK + n_exp
    n_rows = TOP_K * n + nt * n_exp * (MOE_SEG_ALIGN - 1) + n_exp * (MOE_CHUNK + MOE_ROW_BLOCK)
    n_rows = (n_rows + MOE_ROW_BLOCK - 1) // MOE_ROW_BLOCK * MOE_ROW_BLOCK
    h, comb, cnt = moe_route(x, g, r_hi, r_lo, n_exp, tm)
    cnt = cnt[:, 0, :n_exp].astype(jnp.int32)
    ek, rk, off, n_pieces, blk_exp, blk_rows, n_valid = _moe_plan(cnt, n_rows, max_pieces)
    plan = (ek, rk, off, n_pieces)
    a_sorted, g_sorted = moe_compact(plan, h, comb, n_rows, tm, max_pieces)
    y_sorted = moe_experts(blk_exp, blk_rows, n_valid, a_sorted, g_sorted, w_gate, w_up, w_down)
    return moe_combine(plan, x, comb, g_final, y_sorted, n_exp, tm, max_pieces)


def _pad_cols(w, width):
    return jnp.pad(w, ((0, 0), (0, width - w.shape[1])))


def _even_layer(x2d, bsz, seq, rel_bias, norm_mix, w_in, conv_w, norm_cq, norm_ckv, w_uq, w_uk, w_uv, w_qidx,
                w_out, norm_ffn, ffn_gate, ffn_up, ffn_down):
    nh = w_uq.shape[1]
    a_width = conv_w.shape[1]
    c0 = 3 * a_width
    cq_col, ckv_col = c0, c0 + Q_RANK
    kidx_col = ckv_col + KV_RANK
    widx_col = kidx_col + LANES
    w_in_p = jnp.concatenate([
        w_in[:, :kidx_col],
        _pad_cols(w_in[:, kidx_col:kidx_col + IDX_DIM], LANES),
        _pad_cols(w_in[:, kidx_col + IDX_DIM:], LANES)], axis=1).astype(BF16)
    proj = norm_matmul(x2d, norm_mix, w_in_p).reshape(bsz, seq, -1)

    y_a = conv_mixer(proj, conv_w, a_width)

    w_ukt = jnp.transpose(w_uk, (1, 2, 0)).astype(BF16)
    w_qidx_pad = jnp.pad(w_qidx, ((0, 0), (0, 0), (0, LANES - IDX_DIM))).reshape(Q_RANK, IDX_HEADS * LANES)
    qlat, qidx, ckv, ckvt, wt = dsa_prep(proj, norm_cq, norm_ckv,
                                         w_uq.reshape(Q_RANK, nh * HEAD_DIM).astype(BF16), w_ukt,
                                         w_qidx_pad.astype(BF16), cq_col, ckv_col, widx_col)
    w_uvt = jnp.transpose(w_uv, (1, 2, 0)).astype(BF16)
    y_b = dsa_attention(qlat, qidx, wt, proj, ckv, ckvt, _toeplitz_bias(rel_bias), w_uvt, kidx_col,
                        topk=min(TOPK_MAX, seq // 4))

    n = bsz * seq
    x2d = out_proj(y_a.reshape(n, -1), y_b.reshape(n, -1), w_out.astype(BF16), x2d)
    return ffn(x2d, norm_ffn, ffn_gate.astype(BF16), ffn_up.astype(BF16), ffn_down.astype(BF16))


def _odd_layer(x2d, bsz, seq, norm_mix, w_in, norm_v, w_s, b_s, w_out, norm_ffn, router, exp_gate, exp_up,
               exp_down, final_norm):
    c_width = norm_v.shape[0]
    nh = (w_in.shape[1] - 2 * c_width) // (3 * HEAD_DIM)
    d_width = nh * HEAD_DIM
    col = jnp.arange(w_in.shape[1])
    is_q = (col >= 2 * c_width) & (col < 2 * c_width + d_width)
    w_in_s = jnp.where(is_q[None, :], w_in * (HEAD_DIM ** -0.5 * LOG2_E), w_in).astype(BF16)
    proj = norm_matmul(x2d, norm_mix, w_in_s)
    y_c = gmlp_mixer(proj, norm_v, w_s, jnp.transpose(b_s), c_width)
    y_d = stick_breaking(proj.reshape(bsz, seq, -1), nh, 2 * c_width, 2 * c_width + d_width,
                         2 * c_width + 2 * d_width)
    x2d = out_proj(y_c, y_d.reshape(bsz * seq, -1), w_out.astype(BF16), x2d)
    r_pad = _pad_cols(router, LANES)
    r_hi = r_pad.astype(BF16)
    r_lo = (r_pad - r_hi.astype(F32)).astype(BF16)
    return moe(x2d, norm_ffn, r_hi, r_lo, exp_gate, exp_up, exp_down,
               final_norm)


def kernel(x, rel_bias, final_norm, e_norm_mix, e_w_in, e_conv_w, e_norm_cq, e_norm_ckv, e_w_uq, e_w_uk, e_w_uv, e_w_qidx, e_w_out, e_norm_ffn, e_ffn_gate, e_ffn_up, e_ffn_down, o_norm_mix, o_w_in, o_norm_v, o_w_s, o_b_s, o_w_out, o_norm_ffn, o_router, o_exp_gate, o_exp_up, o_exp_down):
    bsz, seq, d = x.shape
    assert e_norm_mix.shape[0] == 1 and o_norm_mix.shape[0] == 1, "one even and one odd layer"
    x2d = x.reshape(bsz * seq, d)
    x2d = _even_layer(x2d, bsz, seq, rel_bias, e_norm_mix[0], e_w_in[0], e_conv_w[0], e_norm_cq[0],
                      e_norm_ckv[0], e_w_uq[0], e_w_uk[0], e_w_uv[0], e_w_qidx[0], e_w_out[0], e_norm_ffn[0],
                      e_ffn_gate[0], e_ffn_up[0], e_ffn_down[0])
    out = _odd_layer(x2d, bsz, seq, o_norm_mix[0], o_w_in[0], o_norm_v[0], o_w_s[0], o_b_s[0], o_w_out[0],
                     o_norm_ffn[0], o_router[0], o_exp_gate[0], o_exp_up[0], o_exp_down[0], final_norm)
    return out.reshape(bsz, seq, d)
```

```python
import functools
import math

import jax
import jax.numpy as jnp
from jax import lax
from jax.experimental import pallas as pl
from jax.experimental.pallas import tpu as pltpu

F32 = jnp.float32
BF16 = jnp.bfloat16

EPS = 1e-6
LANES = 128
COUNT_ROWS = 64
HEAD_DIM = 128
Q_RANK = 512
KV_RANK = 256
IDX_HEADS = 16
IDX_DIM = 64
IDX_SCALE = (IDX_DIM ** -0.5) * (IDX_HEADS ** -0.5)
TOPK_MAX = 256
DSA_KEY_CHUNK = 256
REL_BUCKETS = 32
REL_MAX_DIST = 128
CONV_WIDTH = 3
CHUNK = 128
TOP_K = 2
LOG2_E = math.log2(math.e)
INT_MIN = -(2 ** 31)
MIB = 1024 * 1024


def _params(sem, vmem_mib):
    return pltpu.CompilerParams(dimension_semantics=sem, vmem_limit_bytes=vmem_mib * MIB)


def _rms(x, g):
    return x * lax.rsqrt(jnp.mean(x * x, axis=-1, keepdims=True) + EPS) * g


def _dot(a, b):
    return jnp.dot(a, b, preferred_element_type=F32)


def _dot_nt(a, b):
    return lax.dot_general(a, b, (((1,), (1,)), ((), ())), preferred_element_type=F32)


def _norm_matmul_kernel(x_ref, g_ref, w_ref, o_ref, h_ref):
    @pl.when(pl.program_id(1) == 0)
    def _():
        h_ref[...] = _rms(x_ref[...], g_ref[...]).astype(BF16)

    o_ref[...] = _dot(h_ref[...], w_ref[...]).astype(o_ref.dtype)


def norm_matmul(x, g, w, tm=1024, tn=1024):
    n, d = x.shape
    nout = w.shape[1]
    return pl.pallas_call(
        _norm_matmul_kernel,
        grid=(n // tm, nout // tn),
        in_specs=[pl.BlockSpec((tm, d), lambda i, j: (i, 0)),
                  pl.BlockSpec((1, d), lambda i, j: (0, 0)),
                  pl.BlockSpec((d, tn), lambda i, j: (0, j))],
        out_specs=pl.BlockSpec((tm, tn), lambda i, j: (i, j)),
        out_shape=jax.ShapeDtypeStruct((n, nout), BF16),
        scratch_shapes=[pltpu.VMEM((tm, d), BF16)],
        compiler_params=_params(("parallel", "arbitrary"), 40),
        name="norm_matmul",
    )(x, g.reshape(1, d), w)


def _conv_kernel(b_ref, c_ref, x_ref, w_ref, o_ref):
    z = c_ref[0].astype(F32) * x_ref[0].astype(F32)
    row = lax.broadcasted_iota(jnp.int32, z.shape, 0)
    y = w_ref[CONV_WIDTH - 1:CONV_WIDTH, :] * z
    for lag in range(1, CONV_WIDTH):
        zl = jnp.where(row >= lag, pltpu.roll(z, lag, 0), 0.0)
        y = y + w_ref[CONV_WIDTH - 1 - lag:CONV_WIDTH - lag, :] * zl
    o_ref[0] = (b_ref[0].astype(F32) * y).astype(o_ref.dtype)


def conv_mixer(proj, conv_w, a_width, tc=256):
    bsz, seq, _ = proj.shape
    nb = a_width // tc
    return pl.pallas_call(
        _conv_kernel,
        grid=(bsz, nb),
        in_specs=[pl.BlockSpec((1, seq, tc), lambda b, c: (b, 0, c)),
                  pl.BlockSpec((1, seq, tc), lambda b, c: (b, 0, nb + c)),
                  pl.BlockSpec((1, seq, tc), lambda b, c: (b, 0, 2 * nb + c)),
                  pl.BlockSpec((CONV_WIDTH, tc), lambda b, c: (0, c))],
        out_specs=pl.BlockSpec((1, seq, tc), lambda b, c: (b, 0, c)),
        out_shape=jax.ShapeDtypeStruct((bsz, seq, a_width), BF16),
        compiler_params=_params(("parallel", "parallel"), 40),
        name="conv_mixer",
    )(proj, proj, proj, conv_w)


def _dsa_prep_kernel(cq_ref, ckv_ref, w_ref, gq_ref, gkv_ref, wuq_ref, wukt_ref, wqi_ref,
                     qlat_ref, qidx_ref, ckvo_ref, ckvt_ref, wt_ref):
    tm = cq_ref.shape[1]
    cqn = _rms(cq_ref[0].astype(F32), gq_ref[...]).astype(BF16)
    ckvn = _rms(ckv_ref[0].astype(F32), gkv_ref[...])
    ckvo_ref[0] = ckvn.astype(BF16)
    for c in range(tm // DSA_KEY_CHUNK):
        rows = slice(c * DSA_KEY_CHUNK, (c + 1) * DSA_KEY_CHUNK)
        ckvt_ref[0, c] = jnp.transpose(ckvn[rows, :]).astype(BF16)
    wt_ref[0] = jnp.transpose(w_ref[0].astype(F32))[:IDX_HEADS, :] * IDX_SCALE
    q = _dot(cqn, wuq_ref[...]).astype(BF16)
    scale = HEAD_DIM ** -0.5 * LOG2_E
    for h in range(wukt_ref.shape[0]):
        qh = q[:, h * HEAD_DIM:(h + 1) * HEAD_DIM]
        qlat_ref[0, h] = (_dot(qh, wukt_ref[h]) * scale).astype(BF16)
    qi = _dot(cqn, wqi_ref[...]).astype(BF16)
    for g in range(IDX_HEADS):
        qidx_ref[0, g] = qi[:, g * LANES:(g + 1) * LANES]


def dsa_prep(proj, g_cq, g_ckv, w_uq, w_ukt, w_qidx_pad, cq_col, ckv_col, widx_col, tm=256):
    bsz, seq, _ = proj.shape
    nh = w_ukt.shape[0]
    full = lambda *shape: pl.BlockSpec(shape, lambda b, i: (0,) * len(shape))
    return pl.pallas_call(
        _dsa_prep_kernel,
        grid=(bsz, seq // tm),
        in_specs=[pl.BlockSpec((1, tm, Q_RANK), lambda b, i: (b, i, cq_col // Q_RANK)),
                  pl.BlockSpec((1, tm, KV_RANK), lambda b, i: (b, i, ckv_col // KV_RANK)),
                  pl.BlockSpec((1, tm, LANES), lambda b, i: (b, i, widx_col // LANES)),
                  full(1, Q_RANK), full(1, KV_RANK),
                  full(Q_RANK, nh * HEAD_DIM), full(nh, HEAD_DIM, KV_RANK),
                  full(Q_RANK, IDX_HEADS * LANES)],
        out_specs=[pl.BlockSpec((1, nh, tm, KV_RANK), lambda b, i: (b, 0, i, 0)),
                   pl.BlockSpec((1, IDX_HEADS, tm, LANES), lambda b, i: (b, 0, i, 0)),
                   pl.BlockSpec((1, tm, KV_RANK), lambda b, i: (b, i, 0)),
                   pl.BlockSpec((1, tm // DSA_KEY_CHUNK, KV_RANK, DSA_KEY_CHUNK), lambda b, i: (b, i, 0, 0)),
                   pl.BlockSpec((1, IDX_HEADS, tm), lambda b, i: (b, 0, i))],
        out_shape=[jax.ShapeDtypeStruct((bsz, nh, seq, KV_RANK), BF16),
                   jax.ShapeDtypeStruct((bsz, IDX_HEADS, seq, LANES), BF16),
                   jax.ShapeDtypeStruct((bsz, seq, KV_RANK), BF16),
                   jax.ShapeDtypeStruct((bsz, seq // DSA_KEY_CHUNK, KV_RANK, DSA_KEY_CHUNK), BF16),
                   jax.ShapeDtypeStruct((bsz, IDX_HEADS, seq), F32)],
        compiler_params=_params(("parallel", "parallel"), 40),
        name="dsa_prep",
    )(proj, proj, proj, g_cq.reshape(1, -1), g_ckv.reshape(1, -1), w_uq, w_ukt, w_qidx_pad)


def _dsa_kernel(qlat_ref, qidx_ref, wt_ref, kidx_ref, ckv_ref, ckvt_ref, toep_ref, wuvt_ref, o_ref,
                sc_ref, key_ref, p_ref, m_ref, l_ref, alpha_ref, ties_seen_ref, acc_ref, *, topk):
    tq = LANES
    tk = DSA_KEY_CHUNK
    nh = qlat_ref.shape[1]
    i = pl.program_id(1)
    n_chunks = (i * tq) // tk + 1
    kpos = lax.broadcasted_iota(jnp.int32, (tk, tq), 0)
    qpos = lax.broadcasted_iota(jnp.int32, (tk, tq), 1)
    wt = wt_ref[0]
    qi = qidx_ref[0].reshape(IDX_HEADS * tq, LANES)

    def score_body(c, carry):
        kj = kidx_ref[0, pl.ds(pl.multiple_of(c * tk, tk), tk), :]
        sg = _dot_nt(kj, qi)
        s = jnp.zeros((tk, tq), F32)
        for g in range(IDX_HEADS):
            s = s + wt[g:g + 1, :] * jnp.maximum(sg[:, g * tq:(g + 1) * tq], 0.0)
        sc_ref[c] = s
        bits = pltpu.bitcast(s, jnp.int32)
        key = bits ^ ((bits >> 31) & 0x7FFFFFFF)
        causal = (c * tk + kpos) <= (i * tq + qpos)
        key_ref[c] = jnp.where(causal, key, INT_MIN)
        return carry

    lax.fori_loop(0, n_chunks, score_body, 0)

    kvec = jnp.minimum(topk, i * tq + qpos[:1, :] + 1).astype(F32)

    def count_ge(cand):
        def body(c, acc):
            hit = jnp.where(key_ref[c] >= cand, 1.0, 0.0)
            return acc + jnp.sum(hit.reshape(tk // COUNT_ROWS, COUNT_ROWS, tq), axis=0)
        acc = lax.fori_loop(0, n_chunks, body, jnp.zeros((COUNT_ROWS, tq), F32))
        return jnp.sum(acc, axis=0, keepdims=True)

    thr = jnp.where(count_ge(jnp.zeros((1, tq), jnp.int32)) >= kvec, 0, INT_MIN).astype(jnp.int32)

    def bit_body(b, thr):
        cand = thr | jnp.left_shift(jnp.int32(1), 30 - b)
        return jnp.where(count_ge(cand) >= kvec, cand, thr)

    thr = lax.fori_loop(0, 31, bit_body, thr)
    ties_wanted = kvec - count_ge(thr + 1)
    ties_seen_ref[...] = jnp.zeros(ties_seen_ref.shape, F32)
    kr = lax.broadcasted_iota(jnp.int32, (tk, tk), 0)
    kc = lax.broadcasted_iota(jnp.int32, (tk, tk), 1)
    upto = jnp.where(kc <= kr, 1.0, 0.0).astype(BF16)

    m_ref[...] = jnp.full(m_ref.shape, -jnp.inf, F32)
    l_ref[...] = jnp.zeros(l_ref.shape, F32)
    acc_ref[...] = jnp.zeros(acc_ref.shape, F32)
    q = qlat_ref[0].reshape(nh * tq, KV_RANK)

    def att_body(c, carry):
        cj = ckv_ref[0, pl.ds(pl.multiple_of(c * tk, tk), tk), :]
        key = key_ref[c]
        tie = key == thr
        tie_rank = _dot(upto, jnp.where(tie, 1.0, 0.0).astype(BF16)) + ties_seen_ref[...]
        ties_seen_ref[...] = tie_rank[tk - 1:tk, :]
        sel = (key > thr) | (tie & (tie_rank <= ties_wanted))
        base = jnp.where(sel, sc_ref[c] * LOG2_E, -jnp.inf)
        mdist = [jnp.clip(i - (c * (tk // tq) + r), 0, 2) for r in range(tk // tq)]
        lg = _dot_nt(cj, q)
        for h in range(nh):
            cols = slice(h * tq, (h + 1) * tq)
            bias = jnp.concatenate([toep_ref[h, m] for m in mdist], axis=0)
            lh = lg[:, cols] + bias + base
            m_old = m_ref[:, cols]
            m_new = jnp.maximum(m_old, jnp.max(lh, axis=0, keepdims=True))
            m_safe = jnp.where(m_new == -jnp.inf, 0.0, m_new)
            alpha = jnp.exp2(m_old - m_safe)
            p = jnp.exp2(lh - m_safe)
            m_ref[:, cols] = m_new
            l_ref[:, cols] = alpha * l_ref[:, cols] + jnp.sum(p, axis=0, keepdims=True)
            alpha_ref[:, cols] = alpha
            p_ref[:, cols] = p.astype(BF16)
        acc_ref[...] = alpha_ref[...] * acc_ref[...] + _dot(ckvt_ref[0, c], p_ref[...])
        return carry

    lax.fori_loop(0, n_chunks, att_body, 0)

    for h in range(nh):
        cols = slice(h * tq, (h + 1) * tq)
        o_lat_t = (acc_ref[:, cols] / l_ref[:, cols]).astype(BF16)
        y_t = _dot(wuvt_ref[h], o_lat_t)
        o_ref[0, :, h * HEAD_DIM:(h + 1) * HEAD_DIM] = jnp.transpose(y_t).astype(o_ref.dtype)


def dsa_attention(qlat, qidx, wt, proj, ckv, ckvt, toep, w_uvt, kidx_col, topk):
    bsz, nh, seq, _ = qlat.shape
    tq = LANES
    tk = DSA_KEY_CHUNK
    nkb = seq // tk
    full = lambda *shape: pl.BlockSpec(shape, lambda b, i: (0,) * len(shape))
    return pl.pallas_call(
        functools.partial(_dsa_kernel, topk=topk),
        grid=(bsz, seq // tq),
        in_specs=[pl.BlockSpec((1, nh, tq, KV_RANK), lambda b, i: (b, 0, i, 0)),
                  pl.BlockSpec((1, IDX_HEADS, tq, LANES), lambda b, i: (b, 0, i, 0)),
                  pl.BlockSpec((1, IDX_HEADS, tq), lambda b, i: (b, 0, i)),
                  pl.BlockSpec((1, seq, LANES), lambda b, i: (b, 0, kidx_col // LANES)),
                  pl.BlockSpec((1, seq, KV_RANK), lambda b, i: (b, 0, 0)),
                  pl.BlockSpec((1, nkb, KV_RANK, tk), lambda b, i: (b, 0, 0, 0)),
                  full(nh, 3, tq, tq),
                  full(nh, HEAD_DIM, KV_RANK)],
        out_specs=pl.BlockSpec((1, tq, nh * HEAD_DIM), lambda b, i: (b, i, 0)),
        out_shape=jax.ShapeDtypeStruct((bsz, seq, nh * HEAD_DIM), BF16),
        scratch_shapes=[pltpu.VMEM((nkb, tk, tq), F32),
                        pltpu.VMEM((nkb, tk, tq), jnp.int32),
                        pltpu.VMEM((tk, nh * tq), BF16),
                        pltpu.VMEM((1, nh * tq), F32),
                        pltpu.VMEM((1, nh * tq), F32),
                        pltpu.VMEM((1, nh * tq), F32),
                        pltpu.VMEM((1, tq), F32),
                        pltpu.VMEM((KV_RANK, nh * tq), F32)],
        compiler_params=_params(("parallel", "arbitrary"), 40),
        name="dsa_attention",
    )(qlat, qidx, wt, proj, ckv, ckvt, toep, w_uvt)


def _rel_bucket(dist):
    n = jnp.maximum(dist, 0)
    max_exact = REL_BUCKETS // 2
    nf = jnp.maximum(n, max_exact).astype(F32)
    large = max_exact + (jnp.log(nf / max_exact) / math.log(REL_MAX_DIST / max_exact)
                         * (REL_BUCKETS - max_exact)).astype(jnp.int32)
    large = jnp.minimum(large, REL_BUCKETS - 1)
    return jnp.where(n < max_exact, n, large)


def _toeplitz_bias(rel_bias):
    t = LANES
    r = jnp.arange(t, dtype=jnp.int32)
    tiles = []
    for m in range(3):
        dist = m * t + r[None, :] - r[:, None] if m < 2 else jnp.full((t, t), 2 * REL_MAX_DIST, jnp.int32)
        tiles.append(_rel_bucket(dist))
    bucket = jnp.stack(tiles)
    bias = jnp.zeros((rel_bias.shape[1],) + bucket.shape, F32)
    for k in range(REL_BUCKETS):
        bias = jnp.where(bucket[None] == k, rel_bias[k][:, None, None, None], bias)
    return bias * LOG2_E


def _out_proj_kernel(ya_ref, yb_ref, wa_ref, wb_ref, x_ref, o_ref):
    o_ref[...] = x_ref[...] + _dot(ya_ref[...], wa_ref[...]) + _dot(yb_ref[...], wb_ref[...])


def out_proj(ya, yb, w_out, x, tm=1024, tn=1024):
    n, ka = ya.shape
    kb = yb.shape[1]
    d = x.shape[1]
    wa, wb = w_out[:ka], w_out[ka:]
    return pl.pallas_call(
        _out_proj_kernel,
        grid=(n // tm, d // tn),
        in_specs=[pl.BlockSpec((tm, ka), lambda i, j: (i, 0)),
                  pl.BlockSpec((tm, kb), lambda i, j: (i, 0)),
                  pl.BlockSpec((ka, tn), lambda i, j: (0, j)),
                  pl.BlockSpec((kb, tn), lambda i, j: (0, j)),
                  pl.BlockSpec((tm, tn), lambda i, j: (i, j))],
        out_specs=pl.BlockSpec((tm, tn), lambda i, j: (i, j)),
        out_shape=jax.ShapeDtypeStruct((n, d), F32),
        compiler_params=_params(("parallel", "parallel"), 40),
        name="out_proj",
    )(ya, yb, wa, wb, x)


def _silu(a):
    return a / (1.0 + jnp.exp(-a))


def _ffn_kernel(x_ref, g_ref, wg_ref, wu_ref, wd_ref, o_ref, h_ref):
    @pl.when(pl.program_id(1) == 0)
    def _():
        x = x_ref[...]
        h_ref[...] = _rms(x, g_ref[...]).astype(BF16)
        o_ref[...] = x

    h = h_ref[...]
    act = (_silu(_dot(h, wg_ref[...])) * _dot(h, wu_ref[...])).astype(BF16)
    o_ref[...] += _dot(act, wd_ref[...])


def ffn(x, g, w_gate, w_up, w_down, tm=1024, tf=512):
    n, d = x.shape
    dff = w_gate.shape[1]
    return pl.pallas_call(
        _ffn_kernel,
        grid=(n // tm, dff // tf),
        in_specs=[pl.BlockSpec((tm, d), lambda i, f: (i, 0)),
                  pl.BlockSpec((1, d), lambda i, f: (0, 0)),
                  pl.BlockSpec((d, tf), lambda i, f: (0, f)),
                  pl.BlockSpec((d, tf), lambda i, f: (0, f)),
                  pl.BlockSpec((tf, d), lambda i, f: (f, 0))],
        out_specs=pl.BlockSpec((tm, d), lambda i, f: (i, 0)),
        out_shape=jax.ShapeDtypeStruct((n, d), F32),
        scratch_shapes=[pltpu.VMEM((tm, d), BF16)],
        compiler_params=_params(("parallel", "arbitrary"), 58),
        name="ffn",
    )(x, g.reshape(1, d), w_gate, w_up, w_down)


def _gelu(x):
    return x * (0.5 * (1.0 + jnp.tanh(math.sqrt(2.0 / math.pi) * (x + 0.044715 * (x * x * x)))))


def _gmlp_kernel(cu_ref, cv_ref, gv_ref, ws_ref, bst_ref, o_ref):
    tm = cu_ref.shape[0]
    ng = ws_ref.shape[0]
    gw = cu_ref.shape[1] // ng
    u = _gelu(cu_ref[...].astype(F32))
    vn = _rms(_gelu(cv_ref[...].astype(F32)), gv_ref[...]).astype(BF16)
    row = lax.broadcasted_iota(jnp.int32, (CHUNK, CHUNK), 0)
    col = lax.broadcasted_iota(jnp.int32, (CHUNK, CHUNK), 1)
    for g in range(ng):
        wsg = jnp.where(row >= col, ws_ref[g], 0.0).astype(BF16)
        bias = bst_ref[:, g:g + 1]
        cols = slice(g * gw, (g + 1) * gw)
        for c in range(tm // CHUNK):
            rows = slice(c * CHUNK, (c + 1) * CHUNK)
            mixed = _dot(wsg, vn[rows, cols]) + bias
            o_ref[rows, cols] = (u[rows, cols] * mixed).astype(o_ref.dtype)


def gmlp_mixer(proj, g_v, w_s, b_s_t, c_width, tm=256):
    n = proj.shape[0]
    ng = w_s.shape[0]
    return pl.pallas_call(
        _gmlp_kernel,
        grid=(n // tm,),
        in_specs=[pl.BlockSpec((tm, c_width), lambda i: (i, 0)),
                  pl.BlockSpec((tm, c_width), lambda i: (i, 1)),
                  pl.BlockSpec((1, c_width), lambda i: (0, 0)),
                  pl.BlockSpec((ng, CHUNK, CHUNK), lambda i: (0, 0, 0)),
                  pl.BlockSpec((CHUNK, ng), lambda i: (0, 0))],
        out_specs=pl.BlockSpec((tm, c_width), lambda i: (i, 0)),
        out_shape=jax.ShapeDtypeStruct((n, c_width), BF16),
        compiler_params=_params(("parallel",), 40),
        name="gmlp_mixer",
    )(proj, proj, g_v.reshape(1, -1), w_s, b_s_t)


def _sb_kernel(q_ref, k_ref, v_ref, u_ref, o_ref, acc_ref, carry_ref, *, tk):
    tq = q_ref.shape[1]
    i = pl.program_id(2)
    acc_ref[...] = jnp.zeros(acc_ref.shape, F32)
    carry_ref[...] = jnp.zeros(carry_ref.shape, F32)
    q = q_ref[0]
    tpos = i * tq + lax.broadcasted_iota(jnp.int32, (tq, tk), 0)
    col = lax.broadcasted_iota(jnp.int32, (tq, tk), 1)
    n_diag = tq // tk

    def block(j, diagonal):
        start = pl.multiple_of(j * tk, tk)
        for h in range(q_ref.shape[2] // HEAD_DIM):
            cols = slice(h * HEAD_DIM, (h + 1) * HEAD_DIM)
            kj = k_ref[0, pl.ds(start, tk), cols]
            vj = v_ref[0, pl.ds(start, tk), cols]
            z = _dot_nt(q[:, cols], kj)
            log_keep = -(jnp.maximum(z, 0.0) + jnp.log2(1.0 + jnp.exp2(-jnp.abs(z))))
            log_beta = z + log_keep
            if diagonal:
                past = (j * tk + col) < tpos
                log_keep = jnp.where(past, log_keep, 0.0)
            hi = log_keep.astype(BF16)
            lo = (log_keep - hi.astype(F32)).astype(BF16)
            excl = _dot(hi, u_ref[...]) + _dot(lo, u_ref[...])
            wts = jnp.exp2(log_beta + excl + carry_ref[h])
            if diagonal:
                wts = jnp.where(past, wts, 0.0)
            acc_ref[:, cols] += _dot(wts.astype(BF16), vj)
            carry_ref[h] += jnp.sum(log_keep, axis=1, keepdims=True)

    for d in range(n_diag):
        block((i + 1) * n_diag - 1 - d, True)

    def body(step, c):
        block(i * n_diag - 1 - step, False)
        return c

    lax.fori_loop(0, i * n_diag, body, 0)
    o_ref[0] = acc_ref[...].astype(o_ref.dtype)


def stick_breaking(proj, nh, q_col, k_col, v_col, tq=512, tk=256, heads_per_step=4):
    bsz, seq, _ = proj.shape
    tq = min(tq, seq)
    tk = min(tk, tq)
    r = jnp.arange(tk, dtype=jnp.int32)
    upper = (r[:, None] > r[None, :]).astype(BF16)
    hw = heads_per_step * HEAD_DIM
    qb, kb, vb = q_col // hw, k_col // hw, v_col // hw
    assert q_col % hw == 0 and k_col % hw == 0 and v_col % hw == 0 and nh % heads_per_step == 0
    return pl.pallas_call(
        functools.partial(_sb_kernel, tk=tk),
        grid=(bsz, nh // heads_per_step, seq // tq),
        in_specs=[pl.BlockSpec((1, tq, hw), lambda b, h, i: (b, i, qb + h)),
                  pl.BlockSpec((1, seq, hw), lambda b, h, i: (b, 0, kb + h)),
                  pl.BlockSpec((1, seq, hw), lambda b, h, i: (b, 0, vb + h)),
                  pl.BlockSpec((tk, tk), lambda b, h, i: (0, 0))],
        out_specs=pl.BlockSpec((1, tq, hw), lambda b, h, i: (b, i, h)),
        out_shape=jax.ShapeDtypeStruct((bsz, seq, nh * HEAD_DIM), BF16),
        scratch_shapes=[pltpu.VMEM((tq, hw), F32), pltpu.VMEM((heads_per_step, tq, 1), F32)],
        compiler_params=_params(("parallel", "parallel", "arbitrary"), 40),
        name="stick_breaking",
    )(proj, proj, proj, upper)


MOE_CHUNK = 64
MOE_SEG_ALIGN = 16
MOE_ROW_BLOCK = 1024
MOE_SUB_BLOCK = 256


def _moe_route_kernel(x_ref, g_ref, rhi_ref, rlo_ref, h_ref, comb_ref, cnt_ref, *, n_exp):
    tm = x_ref.shape[0]
    h32 = _rms(x_ref[...], g_ref[...])
    hi = h32.astype(BF16)
    lo = (h32 - hi.astype(F32)).astype(BF16)
    h_ref[...] = hi
    logits = _dot(hi, rhi_ref[...]) + (_dot(hi, rlo_ref[...]) + _dot(lo, rhi_ref[...]))
    lane = lax.broadcasted_iota(jnp.int32, (tm, LANES), 1)
    lanef = lane.astype(F32)
    lg = jnp.where(lane < n_exp, logits, -jnp.inf)
    m1 = jnp.max(lg, axis=1, keepdims=True)
    i1 = jnp.min(jnp.where(lg == m1, lanef, float(LANES)), axis=1, keepdims=True)
    lg2 = jnp.where(lanef == i1, -jnp.inf, lg)
    m2 = jnp.max(lg2, axis=1, keepdims=True)
    i2 = jnp.min(jnp.where(lg2 == m2, lanef, float(LANES)), axis=1, keepdims=True)
    e2 = jnp.exp(m2 - m1)
    g1 = 1.0 / (1.0 + e2)
    comb = jnp.where(lanef == i1, g1, 0.0) + jnp.where(lanef == i2, e2 * g1, 0.0)
    comb_ref[...] = comb
    cnt_ref[0] = jnp.sum(jnp.where(comb > 0.0, 1.0, 0.0), axis=0, keepdims=True)


def moe_route(x, g, r_hi, r_lo, n_exp, tm):
    n, d = x.shape
    nt = n // tm
    return pl.pallas_call(
        functools.partial(_moe_route_kernel, n_exp=n_exp),
        grid=(nt,),
        in_specs=[pl.BlockSpec((tm, d), lambda i: (i, 0)),
                  pl.BlockSpec((1, d), lambda i: (0, 0)),
                  pl.BlockSpec((d, LANES), lambda i: (0, 0)),
                  pl.BlockSpec((d, LANES), lambda i: (0, 0))],
        out_specs=[pl.BlockSpec((tm, d), lambda i: (i, 0)),
                   pl.BlockSpec((tm, LANES), lambda i: (i, 0)),
                   pl.BlockSpec((1, 1, LANES), lambda i: (i, 0, 0))],
        out_shape=[jax.ShapeDtypeStruct((n, d), BF16),
                   jax.ShapeDtypeStruct((n, LANES), F32),
                   jax.ShapeDtypeStruct((nt, 1, LANES), F32)],
        compiler_params=_params(("parallel",), 40),
        name="moe_route",
    )(x, g.reshape(1, d), r_hi, r_lo)


def _moe_plan(cnt, n_rows_static, max_pieces):
    nt, n_exp = cnt.shape
    i32 = jnp.int32
    cp = (cnt + MOE_SEG_ALIGN - 1) // MOE_SEG_ALIGN * MOE_SEG_ALIGN
    reg = (jnp.sum(cp, axis=0) + MOE_CHUNK + MOE_ROW_BLOCK - 1) // MOE_ROW_BLOCK * MOE_ROW_BLOCK
    base = jnp.cumsum(reg) - reg
    seg = base[None, :] + jnp.cumsum(cp, axis=0) - cp
    nce = (cnt + MOE_CHUNK - 1) // MOE_CHUNK
    cum = jnp.cumsum(nce, axis=1)
    n_pieces = cum[:, -1]
    k = jnp.arange(max_pieces, dtype=i32)
    ek = jnp.minimum(jnp.sum(cum[:, None, :] <= k[None, :, None], axis=-1), n_exp - 1).astype(i32)
    is_e = ek[:, :, None] == jnp.arange(n_exp, dtype=i32)
    pick = lambda tbl: jnp.sum(jnp.where(is_e, tbl[:, None, :], 0), axis=-1)
    rk = k[None, :] - pick(cum - nce)
    off = pick(seg) + rk * MOE_CHUNK
    valid = k[None, :] < n_pieces[:, None]
    rk = jnp.where(valid, rk, 0).astype(i32)
    off = jnp.where(valid, off, 0).astype(i32)
    n_blocks = n_rows_static // MOE_ROW_BLOCK
    cb = jnp.cumsum(reg // MOE_ROW_BLOCK)
    blk = jnp.arange(n_blocks, dtype=i32)
    blk_exp = jnp.minimum(jnp.sum(cb[None, :] <= blk[:, None], axis=1), n_exp - 1)
    is_be = blk_exp[:, None] == jnp.arange(n_exp, dtype=i32)
    pick_b = lambda vec: jnp.sum(jnp.where(is_be, vec[None, :], 0), axis=-1)
    used = jnp.sum(cp, axis=0)
    blk_rows = jnp.clip(pick_b(used) - (blk * MOE_ROW_BLOCK - pick_b(base)), 0, MOE_ROW_BLOCK)
    return (ek.reshape(-1), rk.reshape(-1), off.reshape(-1), n_pieces.astype(i32),
            blk_exp.astype(i32), blk_rows.astype(i32), cb[-1:].astype(i32))


def _moe_compact_kernel(ek_ref, rk_ref, off_ref, np_ref, h_ref, comb_ref, a_in, g_in, a_out, g_out,
                        pos_ref, p_ref, stage_ref, gst_ref, sem, *, max_pieces):
    del a_in, g_in
    i = pl.program_id(0)
    tm = h_ref.shape[0]
    n_pieces = np_ref[i]
    comb = comb_ref[...]
    mask_t = jnp.transpose(jnp.where(comb > 0.0, 1.0, 0.0))
    r0 = lax.broadcasted_iota(jnp.int32, (tm, tm), 0)
    r1 = lax.broadcasted_iota(jnp.int32, (tm, tm), 1)
    before = jnp.where(r0 < r1, 1.0, 0.0).astype(BF16)
    pos_t = _dot(mask_t.astype(BF16), before)
    pos_ref[...] = jnp.where(mask_t > 0.0, pos_t, -1.0)
    slot = lax.broadcasted_iota(jnp.int32, (MOE_CHUNK, tm), 0).astype(F32)
    for k in range(max_pieces):
        rows = slice(k * MOE_CHUNK, (k + 1) * MOE_CHUNK)
        e_k = ek_ref[i * max_pieces + k]
        r_k = rk_ref[i * max_pieces + k]

        @pl.when(k < n_pieces)
        def _():
            want = slot + (r_k * MOE_CHUNK).astype(F32)
            p_ref[rows] = jnp.where(pos_ref[pl.ds(e_k, 1), :] == want, 1.0, 0.0).astype(BF16)

        @pl.when(k >= n_pieces)
        def _():
            p_ref[rows] = jnp.zeros((MOE_CHUNK, tm), BF16)

    d = h_ref.shape[1]
    nc = 512
    for c in range(d // nc):
        stage_ref[:, c * nc:(c + 1) * nc] = _dot(p_ref[...], h_ref[:, c * nc:(c + 1) * nc]).astype(BF16)
    c_hi = comb.astype(BF16)
    c_mid = (comb - c_hi.astype(F32)).astype(BF16)
    c_lo = (comb - c_hi.astype(F32) - c_mid.astype(F32)).astype(BF16)
    p = p_ref[...]
    gst_ref[...] = _dot(p, c_hi) + _dot(p, c_mid) + _dot(p, c_lo)

    def copies(k):
        rows = pl.ds(k * MOE_CHUNK, MOE_CHUNK)
        dst = pl.ds(pl.multiple_of(off_ref[i * max_pieces + k], MOE_SEG_ALIGN), MOE_CHUNK)
        return (pltpu.make_async_copy(stage_ref.at[rows], a_out.at[dst], sem.at[0]),
                pltpu.make_async_copy(gst_ref.at[rows], g_out.at[dst], sem.at[1]))

    for k in range(max_pieces):
        @pl.when(k < n_pieces)
        def _():
            for cp in copies(k):
                cp.start()
    for k in range(max_pieces):
        @pl.when(k < n_pieces)
        def _():
            for cp in copies(k):
                cp.wait()


def moe_compact(plan, h, comb, n_rows, tm, max_pieces):
    ek, rk, off, n_pieces = plan
    n, d = h.shape
    a0 = jnp.zeros((n_rows, d), BF16)
    g0 = jnp.zeros((n_rows, LANES), F32)
    any_spec = pl.BlockSpec(memory_space=pl.ANY)
    return pl.pallas_call(
        functools.partial(_moe_compact_kernel, max_pieces=max_pieces),
        grid_spec=pltpu.PrefetchScalarGridSpec(
            num_scalar_prefetch=4,
            grid=(n // tm,),
            in_specs=[pl.BlockSpec((tm, d), lambda i, *_: (i, 0)),
                      pl.BlockSpec((tm, LANES), lambda i, *_: (i, 0)),
                      any_spec, any_spec],
            out_specs=[any_spec, any_spec],
            scratch_shapes=[pltpu.VMEM((LANES, tm), F32),
                            pltpu.VMEM((max_pieces * MOE_CHUNK, tm), BF16),
                            pltpu.VMEM((max_pieces * MOE_CHUNK, d), BF16),
                            pltpu.VMEM((max_pieces * MOE_CHUNK, LANES), F32),
                            pltpu.SemaphoreType.DMA((2,))]),
        out_shape=[jax.ShapeDtypeStruct((n_rows, d), BF16), jax.ShapeDtypeStruct((n_rows, LANES), F32)],
        input_output_aliases={6: 0, 7: 1},
        compiler_params=_params(("arbitrary",), 48),
        name="moe_compact",
    )(ek, rk, off, n_pieces, h, comb, a0, g0)


def _moe_expert_kernel(be_ref, br_ref, nv_ref, a_ref, gs_ref, wg_ref, wu_ref, wd_ref, y_ref, acc_ref):
    b = pl.program_id(0)
    f = pl.program_id(1)
    rb = a_ref.shape[0]
    n_rows = br_ref[b]
    expert = be_ref[b]
    busy = b < nv_ref[0]

    def step(rows):
        @pl.when(f == 0)
        def _():
            acc_ref[rows] = jnp.zeros((rows.stop - rows.start, acc_ref.shape[1]), F32)

        a = a_ref[rows]
        act = (_silu(_dot(a, wg_ref[0].astype(BF16))) * _dot(a, wu_ref[0].astype(BF16))).astype(BF16)
        acc_ref[rows] += _dot(act, wd_ref[0].astype(BF16))

        @pl.when(f == pl.num_programs(1) - 1)
        def _():
            gs = gs_ref[rows]
            lane = lax.broadcasted_iota(jnp.int32, gs.shape, 1)
            gate = jnp.sum(jnp.where(lane == expert, gs, 0.0), axis=1, keepdims=True)
            y_ref[rows] = (acc_ref[rows] * gate).astype(y_ref.dtype)

    @pl.when(busy & (n_rows == rb))
    def _():
        step(slice(0, rb))

    for sb in range(rb // MOE_SUB_BLOCK):
        rows = slice(sb * MOE_SUB_BLOCK, (sb + 1) * MOE_SUB_BLOCK)

        @pl.when(busy & (n_rows < rb) & (sb * MOE_SUB_BLOCK < n_rows))
        def _():
            step(rows)

        @pl.when((jnp.logical_not(busy) | (sb * MOE_SUB_BLOCK >= n_rows)) & (f == 0))
        def _():
            y_ref[rows] = jnp.zeros((MOE_SUB_BLOCK, y_ref.shape[1]), y_ref.dtype)


def moe_experts(blk_exp, blk_rows, n_valid, a_sorted, g_sorted, w_gate, w_up, w_down, tf=512):
    n_rows, d = a_sorted.shape
    n_exp, _, dff = w_gate.shape
    nf = dff // tf
    rb = MOE_ROW_BLOCK

    def blk(b, nv):
        return jnp.minimum(b, nv[0] - 1)

    def fblk(b, f, nv):
        return jnp.where(b < nv[0], f, nf - 1)

    return pl.pallas_call(
        _moe_expert_kernel,
        grid_spec=pltpu.PrefetchScalarGridSpec(
            num_scalar_prefetch=3,
            grid=(n_rows // rb, nf),
            in_specs=[pl.BlockSpec((rb, d), lambda b, f, be, br, nv: (blk(b, nv), 0)),
                      pl.BlockSpec((rb, LANES), lambda b, f, be, br, nv: (blk(b, nv), 0)),
                      pl.BlockSpec((1, d, tf), lambda b, f, be, br, nv: (be[blk(b, nv)], 0, fblk(b, f, nv))),
                      pl.BlockSpec((1, d, tf), lambda b, f, be, br, nv: (be[blk(b, nv)], 0, fblk(b, f, nv))),
                      pl.BlockSpec((1, tf, d), lambda b, f, be, br, nv: (be[blk(b, nv)], fblk(b, f, nv), 0))],
            out_specs=pl.BlockSpec((rb, d), lambda b, f, be, br, nv: (b, 0)),
            scratch_shapes=[pltpu.VMEM((rb, d), F32)]),
        out_shape=jax.ShapeDtypeStruct((n_rows, d), BF16),
        compiler_params=_params(("arbitrary", "arbitrary"), 58),
        name="moe_experts",
    )(blk_exp, blk_rows, n_valid, a_sorted, g_sorted, w_gate, w_up, w_down)


def _moe_combine_kernel(ek_ref, rk_ref, off_ref, np_ref, x_ref, comb_ref, gf_ref, y_hbm, o_ref,
                        posb_ref, sel_ref, ybuf_ref, sem, *, max_pieces, n_exp):
    i = pl.program_id(0)
    tm = x_ref.shape[0]
    n_pieces = np_ref[i]
    slot = i % 2

    def copy(tile, k, half):
        src = pl.ds(pl.multiple_of(off_ref[tile * max_pieces + k], MOE_SEG_ALIGN), MOE_CHUNK)
        return pltpu.make_async_copy(y_hbm.at[src], ybuf_ref.at[half, pl.ds(k * MOE_CHUNK, MOE_CHUNK)],
                                     sem.at[half])

    def fetch(tile, half):
        for k in range(max_pieces):
            @pl.when(k < np_ref[tile])
            def _():
                copy(tile, k, half).start()

            @pl.when(k >= np_ref[tile])
            def _():
                ybuf_ref[half, k * MOE_CHUNK:(k + 1) * MOE_CHUNK, :] = jnp.zeros(
                    (MOE_CHUNK, ybuf_ref.shape[2]), BF16)

    @pl.when(i == 0)
    def _():
        fetch(i, slot)

    @pl.when(i + 1 < pl.num_programs(0))
    def _():
        fetch(i + 1, 1 - slot)

    mask = jnp.where(comb_ref[...] > 0.0, 1.0, 0.0)
    r0 = lax.broadcasted_iota(jnp.int32, (tm, tm), 0)
    r1 = lax.broadcasted_iota(jnp.int32, (tm, tm), 1)
    before = jnp.where(r1 < r0, 1.0, 0.0).astype(BF16)
    pos = jnp.where(mask > 0.0, _dot(before, mask.astype(BF16)), -1.0)
    for e in range(n_exp):
        posb_ref[e] = jnp.broadcast_to(pos[:, e:e + 1], (tm, LANES))
    per_tile = LANES // MOE_CHUNK
    lane = lax.broadcasted_iota(jnp.int32, (tm, LANES), 1)
    for kk in range(max_pieces // per_tile):
        hit = None
        for sub in range(per_tile):
            k = kk * per_tile + sub
            e_k = ek_ref[i * max_pieces + k]
            first = jnp.where(k < n_pieces, rk_ref[i * max_pieces + k] * MOE_CHUNK - sub * MOE_CHUNK, -2 - LANES)
            here = (lane >= sub * MOE_CHUNK) & (lane < (sub + 1) * MOE_CHUNK)
            match = here & (posb_ref[e_k] == (lane + first).astype(F32))
            hit = match if hit is None else (hit | match)
        sel_ref[:, kk * LANES:(kk + 1) * LANES] = jnp.where(hit, 1.0, 0.0).astype(BF16)

    for k in range(max_pieces):
        @pl.when(k < n_pieces)
        def _():
            copy(i, k, slot).wait()

    y = x_ref[...] + _dot(sel_ref[...], ybuf_ref[slot])
    o_ref[...] = _rms(y, gf_ref[...])


def moe_combine(plan, x, comb, g_final, y_sorted, n_exp, tm, max_pieces):
    ek, rk, off, n_pieces = plan
    n, d = x.shape
    return pl.pallas_call(
        functools.partial(_moe_combine_kernel, max_pieces=max_pieces, n_exp=n_exp),
        grid_spec=pltpu.PrefetchScalarGridSpec(
            num_scalar_prefetch=4,
            grid=(n // tm,),
            in_specs=[pl.BlockSpec((tm, d), lambda i, *_: (i, 0)),
                      pl.BlockSpec((tm, LANES), lambda i, *_: (i, 0)),
                      pl.BlockSpec((1, d), lambda i, *_: (0, 0)),
                      pl.BlockSpec(memory_space=pl.ANY)],
            out_specs=pl.BlockSpec((tm, d), lambda i, *_: (i, 0)),
            scratch_shapes=[pltpu.VMEM((n_exp, tm, LANES), F32),
                            pltpu.VMEM((tm, max_pieces * MOE_CHUNK), BF16),
                            pltpu.VMEM((2, max_pieces * MOE_CHUNK, d), BF16),
                            pltpu.SemaphoreType.DMA((2,))]),
        out_shape=jax.ShapeDtypeStruct((n, d), F32),
        compiler_params=_params(("arbitrary",), 56),
        name="moe_combine",
    )(ek, rk, off, n_pieces, x, comb, g_final.reshape(1, d), y_sorted)


def moe(x, g, r_hi, r_lo, w_gate, w_up, w_down, g_final, tm=512):
    n, d = x.shape
    n_exp = w_gate.shape[0]
    tm = min(tm, n // 2)
    nt = n // tm
    max_pieces = TOP_K * tm // MOE_CHUNK + n_exp
    n_rows = TOP_K * n + nt * n_exp * (MOE_SEG_ALIGN - 1) + n_exp * (MOE_CHUNK + MOE_ROW_BLOCK)
    n_rows = (n_rows + MOE_ROW_BLOCK - 1) // MOE_ROW_BLOCK * MOE_ROW_BLOCK
    h, comb, cnt = moe_route(x, g, r_hi, r_lo, n_exp, tm)
    cnt = cnt[:, 0, :n_exp].astype(jnp.int32)
    ek, rk, off, n_pieces, blk_exp, blk_rows, n_valid = _moe_plan(cnt, n_rows, max_pieces)
    plan = (ek, rk, off, n_pieces)
    a_sorted, g_sorted = moe_compact(plan, h, comb, n_rows, tm, max_pieces)
    y_sorted = moe_experts(blk_exp, blk_rows, n_valid, a_sorted, g_sorted, w_gate, w_up, w_down)
    return moe_combine(plan, x, comb, g_final, y_sorted, n_exp, tm, max_pieces)


def _pad_cols(w, width):
    return jnp.pad(w, ((0, 0), (0, width - w.shape[1])))


def _even_layer(x2d, bsz, seq, rel_bias, norm_mix, w_in, conv_w, norm_cq, norm_ckv, w_uq, w_uk, w_uv, w_qidx,
                w_out, norm_ffn, ffn_gate, ffn_up, ffn_down):
    nh = w_uq.shape[1]
    a_width = conv_w.shape[1]
    c0 = 3 * a_width
    cq_col, ckv_col = c0, c0 + Q_RANK
    kidx_col = ckv_col + KV_RANK
    widx_col = kidx_col + LANES
    w_in_p = jnp.concatenate([
        w_in[:, :kidx_col],
        _pad_cols(w_in[:, kidx_col:kidx_col + IDX_DIM], LANES),
        _pad_cols(w_in[:, kidx_col + IDX_DIM:], LANES)], axis=1).astype(BF16)
    proj = norm_matmul(x2d, norm_mix, w_in_p).reshape(bsz, seq, -1)

    y_a = conv_mixer(proj, conv_w, a_width)

    w_ukt = jnp.transpose(w_uk, (1, 2, 0)).astype(BF16)
    w_qidx_pad = jnp.pad(w_qidx, ((0, 0), (0, 0), (0, LANES - IDX_DIM))).reshape(Q_RANK, IDX_HEADS * LANES)
    qlat, qidx, ckv, ckvt, wt = dsa_prep(proj, norm_cq, norm_ckv,
                                         w_uq.reshape(Q_RANK, nh * HEAD_DIM).astype(BF16), w_ukt,
                                         w_qidx_pad.astype(BF16), cq_col, ckv_col, widx_col)
    w_uvt = jnp.transpose(w_uv, (1, 2, 0)).astype(BF16)
    y_b = dsa_attention(qlat, qidx, wt, proj, ckv, ckvt, _toeplitz_bias(rel_bias), w_uvt, kidx_col,
                        topk=min(TOPK_MAX, seq // 4))

    n = bsz * seq
    x2d = out_proj(y_a.reshape(n, -1), y_b.reshape(n, -1), w_out.astype(BF16), x2d)
    return ffn(x2d, norm_ffn, ffn_gate.astype(BF16), ffn_up.astype(BF16), ffn_down.astype(BF16))


def _odd_layer(x2d, bsz, seq, norm_mix, w_in, norm_v, w_s, b_s, w_out, norm_ffn, router, exp_gate, exp_up,
               exp_down, final_norm):
    c_width = norm_v.shape[0]
    nh = (w_in.shape[1] - 2 * c_width) // (3 * HEAD_DIM)
    d_width = nh * HEAD_DIM
    col = jnp.arange(w_in.shape[1])
    is_q = (col >= 2 * c_width) & (col < 2 * c_width + d_width)
    w_in_s = jnp.where(is_q[None, :], w_in * (HEAD_DIM ** -0.5 * LOG2_E), w_in).astype(BF16)
    proj = norm_matmul(x2d, norm_mix, w_in_s)
    y_c = gmlp_mixer(proj, norm_v, w_s, jnp.transpose(b_s), c_width)
    y_d = stick_breaking(proj.reshape(bsz, seq, -1), nh, 2 * c_width, 2 * c_width + d_width,
                         2 * c_width + 2 * d_width)
    x2d = out_proj(y_c, y_d.reshape(bsz * seq, -1), w_out.astype(BF16), x2d)
    r_pad = _pad_cols(router, LANES)
    r_hi = r_pad.astype(BF16)
    r_lo = (r_pad - r_hi.astype(F32)).astype(BF16)
    return moe(x2d, norm_ffn, r_hi, r_lo, exp_gate, exp_up, exp_down,
               final_norm)


def kernel(x, rel_bias, final_norm, e_norm_mix, e_w_in, e_conv_w, e_norm_cq, e_norm_ckv, e_w_uq, e_w_uk, e_w_uv, e_w_qidx, e_w_out, e_norm_ffn, e_ffn_gate, e_ffn_up, e_ffn_down, o_norm_mix, o_w_in, o_norm_v, o_w_s, o_b_s, o_w_out, o_norm_ffn, o_router, o_exp_gate, o_exp_up, o_exp_down):
    bsz, seq, d = x.shape
    assert e_norm_mix.shape[0] == 1 and o_norm_mix.shape[0] == 1, "one even and one odd layer"
    x2d = x.reshape(bsz * seq, d)
    x2d = _even_layer(x2d, bsz, seq, rel_bias, e_norm_mix[0], e_w_in[0], e_conv_w[0], e_norm_cq[0],
                      e_norm_ckv[0], e_w_uq[0], e_w_uk[0], e_w_uv[0], e_w_qidx[0], e_w_out[0], e_norm_ffn[0],
                      e_ffn_gate[0], e_ffn_up[0], e_ffn_down[0])
    out = _odd_layer(x2d, bsz, seq, o_norm_mix[0], o_w_in[0], o_norm_v[0], o_w_s[0], o_b_s[0], o_w_out[0],
                     o_norm_ffn[0], o_router[0], o_exp_gate[0], o_exp_up[0], o_exp_down[0], final_norm)
    return out.reshape(bsz, seq, d)
```
